```python
import math
import jax
import jax.numpy as jnp
from jax import lax
import numpy as np

D_MODEL = 1024
BATCH = 4
SEQ = 4096
DEPTH = 2

SB_HEADS = 4
SB_DIM = 64
DIFF_HEADS = 4
DIFF_DIM = 32
DIFF_V_DIM = 2 * DIFF_DIM
FOX_HEADS = 4
FOX_DIM = 64
MLA_HEADS = 4
MLA_Q_RANK = 256
MLA_KV_RANK = 128
MLA_NOPE_DIM = 64
MLA_ROPE_DIM = 32
MLA_V_DIM = 64
MIX_WIDTH = SB_HEADS * SB_DIM + DIFF_HEADS * DIFF_V_DIM + FOX_HEADS * FOX_DIM + MLA_HEADS * MLA_V_DIM
N_EXPERTS = 32
TOP_K = 4
D_FF = 1024
SWIGLU_LIMIT = 7.0
SWIGLU_ALPHA = 1.702
ROPE_THETA = 10000.0
Q_BLOCK = 128
MOE_BLOCK = 256
LN_EPS = 1e-5
RMS_EPS = 1e-6
DEEPNORM_ALPHA = (2 * DEPTH) ** 0.25
DEEPNORM_BETA = (8 * DEPTH) ** -0.25
IN_SPLITS = (SB_HEADS * SB_DIM, SB_HEADS * SB_DIM, SB_HEADS * SB_DIM,
             DIFF_HEADS * 2 * DIFF_DIM, DIFF_HEADS * 2 * DIFF_DIM, DIFF_HEADS * DIFF_V_DIM,
             FOX_HEADS * FOX_DIM, FOX_HEADS * FOX_DIM, FOX_HEADS * FOX_DIM, FOX_HEADS,
             MLA_Q_RANK, MLA_KV_RANK, MLA_ROPE_DIM)
IN_COLS = sum(IN_SPLITS)

kernel_name = 'hybrid_sb_diff_fox_mla_moe_deepnorm'


def _layer_norm(x, g, b):
    xf = x.astype(jnp.float32)
    mu = jnp.mean(xf, axis=-1, keepdims=True)
    var = jnp.mean(jnp.square(xf - mu), axis=-1, keepdims=True)
    y = (xf - mu) * lax.rsqrt(var + LN_EPS) * g.astype(jnp.float32) + b.astype(jnp.float32)
    return y.astype(x.dtype)


def _rms_norm(x, g):
    xf = x.astype(jnp.float32)
    y = xf * lax.rsqrt(jnp.mean(jnp.square(xf), axis=-1, keepdims=True) + RMS_EPS) * g.astype(jnp.float32)
    return y.astype(x.dtype)


def _rope(x, positions):
    d = x.shape[-1]
    inv_freq = ROPE_THETA ** (-jnp.arange(0, d, 2, dtype=jnp.float32) / d)
    ang = positions.astype(jnp.float32)[..., None] * inv_freq
    cos = jnp.cos(ang)[:, :, None, :]
    sin = jnp.sin(ang)[:, :, None, :]
    xf = x.astype(jnp.float32)
    x1, x2 = xf[..., : d // 2], xf[..., d // 2:]
    return jnp.concatenate([x1 * cos - x2 * sin, x2 * cos + x1 * sin], axis=-1).astype(x.dtype)


def _rows(t, q0):
    return lax.dynamic_slice_in_dim(t, q0, Q_BLOCK, axis=1)


def _causal_mask(q0, seq_len, strict):
    t = q0 + jnp.arange(Q_BLOCK)
    s = jnp.arange(seq_len)
    return (s[None, :] < t[:, None]) if strict else (s[None, :] <= t[:, None])


def _masked_softmax(logits, mask):
    return jax.nn.softmax(jnp.where(mask, logits, -jnp.inf), axis=-1)


def _sweep_query_blocks(block_fn, seq_len):
    starts = jnp.arange(seq_len // Q_BLOCK, dtype=jnp.int32) * Q_BLOCK
    out = lax.map(block_fn, starts)
    nb, b, qb, h, dv = out.shape
    return jnp.transpose(out, (1, 0, 2, 3, 4)).reshape(b, nb * qb, h, dv)


def _stick_breaking(q, k, v):
    seq_len = q.shape[1]
    scale = 1.0 / math.sqrt(q.shape[-1])

    def block(q0):
        z = jnp.einsum('bqhd,bkhd->bhqk', _rows(q, q0), k).astype(jnp.float32) * scale
        strict = _causal_mask(q0, seq_len, strict=True)
        log_keep = jnp.where(strict, jax.nn.log_sigmoid(-z), 0.0)
        after = lax.cumsum(log_keep, axis=3, reverse=True) - log_keep
        w = jnp.where(strict, jnp.exp(jax.nn.log_sigmoid(z) + after), 0.0)
        return jnp.einsum('bhqk,bkhd->bqhd', w.astype(v.dtype), v)

    return _sweep_query_blocks(block, seq_len)


def _differential(q1, q2, k1, k2, v, lam):
    seq_len = q1.shape[1]
    scale = 1.0 / math.sqrt(q1.shape[-1])

    def block(q0):
        mask = _causal_mask(q0, seq_len, strict=False)
        s1 = jnp.einsum('bqhd,bkhd->bhqk', _rows(q1, q0), k1).astype(jnp.float32) * scale
        s2 = jnp.einsum('bqhd,bkhd->bhqk', _rows(q2, q0), k2).astype(jnp.float32) * scale
        w = _masked_softmax(s1, mask) - lam * _masked_softmax(s2, mask)
        return jnp.einsum('bhqk,bkhd->bqhd', w.astype(v.dtype), v)

    return _sweep_query_blocks(block, seq_len)


def _forgetting(q, k, v, cum_log_f):
    seq_len = q.shape[1]
    scale = 1.0 / math.sqrt(q.shape[-1])
    f_key = jnp.transpose(cum_log_f, (0, 2, 1))[:, :, None, :]

    def block(q0):
        mask = _causal_mask(q0, seq_len, strict=False)
        f_q = jnp.transpose(_rows(cum_log_f, q0), (0, 2, 1))[:, :, :, None]
        s = jnp.einsum('bqhd,bkhd->bhqk', _rows(q, q0), k).astype(jnp.float32) * scale + f_q - f_key
        return jnp.einsum('bhqk,bkhd->bqhd', _masked_softmax(s, mask).astype(v.dtype), v)

    return _sweep_query_blocks(block, seq_len)


def _mla(q_nope, q_rope, k_nope, k_rope, v):
    seq_len = q_nope.shape[1]
    scale = 1.0 / math.sqrt(q_nope.shape[-1] + q_rope.shape[-1])

    def block(q0):
        mask = _causal_mask(q0, seq_len, strict=False)
        s = (jnp.einsum('bqhd,bkhd->bhqk', _rows(q_nope, q0), k_nope)
             + jnp.einsum('bqhr,bkr->bhqk', _rows(q_rope, q0), k_rope)).astype(jnp.float32) * scale
        return jnp.einsum('bhqk,bkhd->bqhd', _masked_softmax(s, mask).astype(v.dtype), v)

    return _sweep_query_blocks(block, seq_len)


def _hybrid_mixer(h, positions, lambda_init, w_in, b_forget, diff_lambda, diff_subln_g,
                  mla_q_norm_g, mla_w_uq, mla_kv_norm_g, mla_w_ukv, w_out):
    b, s, _ = h.shape
    split_points = np.cumsum(IN_SPLITS)[:-1].tolist()
    (sb_q, sb_k, sb_v, df_q, df_k, df_v, fx_q, fx_k, fx_v, fx_f,
     mla_cq, mla_ckv, mla_kr) = jnp.split(h @ w_in, split_points, axis=-1)

    def heads(t, n):
        return t.reshape(b, s, n, -1)

    out_a = _stick_breaking(heads(sb_q, SB_HEADS), heads(sb_k, SB_HEADS), heads(sb_v, SB_HEADS))

    dq = _rope(heads(df_q, 2 * DIFF_HEADS), positions).reshape(b, s, DIFF_HEADS, 2, DIFF_DIM)
    dk = _rope(heads(df_k, 2 * DIFF_HEADS), positions).reshape(b, s, DIFF_HEADS, 2, DIFF_DIM)
    lp = diff_lambda.astype(jnp.float32)
    lam = jnp.exp(jnp.sum(lp[0] * lp[1])) - jnp.exp(jnp.sum(lp[2] * lp[3])) + lambda_init
    out_b = _differential(dq[:, :, :, 0], dq[:, :, :, 1], dk[:, :, :, 0], dk[:, :, :, 1],
                          heads(df_v, DIFF_HEADS), lam)
    out_b = _rms_norm(out_b, diff_subln_g) * (1.0 - lambda_init)

    log_f = jax.nn.log_sigmoid((fx_f + b_forget).astype(jnp.float32))
    out_c = _forgetting(heads(fx_q, FOX_HEADS), heads(fx_k, FOX_HEADS), heads(fx_v, FOX_HEADS),
                        jnp.cumsum(log_f, axis=1))

    q_d = heads(_rms_norm(mla_cq, mla_q_norm_g) @ mla_w_uq, MLA_HEADS)
    q_nope, q_rope = q_d[..., :MLA_NOPE_DIM], _rope(q_d[..., MLA_NOPE_DIM:], positions)
    kv_d = heads(_rms_norm(mla_ckv, mla_kv_norm_g) @ mla_w_ukv, MLA_HEADS)
    k_nope, v_d = kv_d[..., :MLA_NOPE_DIM], kv_d[..., MLA_NOPE_DIM:]
    k_rope = _rope(mla_kr[:, :, None, :], positions)[:, :, 0, :]
    out_d = _mla(q_nope, q_rope, k_nope, k_rope, v_d)

    mixed = jnp.concatenate([out_a.reshape(b, s, -1), out_b.reshape(b, s, -1),
                             out_c.reshape(b, s, -1), out_d.reshape(b, s, -1)], axis=-1)
    return mixed @ w_out


def _clamped_swiglu(gu):
    glu, lin = gu[..., :D_FF], gu[..., D_FF:]
    glu = jnp.minimum(glu, SWIGLU_LIMIT)
    lin = jnp.clip(lin, -SWIGLU_LIMIT, SWIGLU_LIMIT)
    return glu * jax.nn.sigmoid(SWIGLU_ALPHA * glu) * (lin + 1.0)


def _moe(h, router_w, router_b, w_gate_up, b_gate_up, w_down, b_down):
    b, s, d = h.shape
    n_tok = b * s
    xt = h.reshape(n_tok, d)
    logits = (xt @ router_w + router_b).astype(jnp.float32)
    top_logit, top_exp = lax.top_k(logits, TOP_K)
    gate = jax.nn.softmax(top_logit, axis=-1)
    n_assign = n_tok * TOP_K
    flat_exp = top_exp.reshape(-1)
    flat_tok = jnp.arange(n_assign, dtype=jnp.int32) // TOP_K
    order = jnp.argsort(flat_exp)
    sorted_exp = flat_exp[order]
    sorted_tok = flat_tok[order]
    sorted_gate = gate.reshape(-1)[order]
    counts = jnp.bincount(flat_exp, length=N_EXPERTS)
    start = jnp.cumsum(counts) - counts
    padded = (counts + MOE_BLOCK - 1) // MOE_BLOCK * MOE_BLOCK
    padded_end = jnp.cumsum(padded)
    padded_start = padded_end - padded
    slot = padded_start[sorted_exp] + (jnp.arange(n_assign, dtype=jnp.int32) - start[sorted_exp])
    n_blocks = n_assign // MOE_BLOCK + N_EXPERTS
    n_slots = n_blocks * MOE_BLOCK
    slot_tok = jnp.full((n_slots,), n_tok, jnp.int32).at[slot].set(sorted_tok)
    x_pad = jnp.concatenate([xt, jnp.zeros((1, d), xt.dtype)], axis=0)
    x_slots = x_pad[slot_tok].reshape(n_blocks, MOE_BLOCK, d)
    block_start = jnp.arange(n_blocks, dtype=jnp.int32) * MOE_BLOCK
    block_exp = jnp.minimum(jnp.sum(padded_end[None, :] <= block_start[:, None], axis=1), N_EXPERTS - 1)

    def expert_block(args):
        xb, e = args
        gu = xb @ w_gate_up[e] + b_gate_up[e]
        return _clamped_swiglu(gu) @ w_down[e] + b_down[e]

    y_slots = lax.map(expert_block, (x_slots, block_exp)).reshape(n_slots, d)
    y_assign = y_slots[slot] * sorted_gate[:, None].astype(h.dtype)
    y = jax.ops.segment_sum(y_assign, sorted_tok, num_segments=n_tok)
    return y.reshape(b, s, d)


def setup_inputs(seed: int = 0) -> dict:
    key = jax.random.key(seed)
    ks = jax.random.split(key, 24)
    f32 = jnp.float32

    def nrm(k, shape, scale):
        return jax.random.normal(k, shape, f32) * scale

    def gain(k, shape):
        return 1.0 + 0.02 * jax.random.normal(k, shape, f32)

    return {
        'x': nrm(ks[0], (BATCH, SEQ, D_MODEL), 1.0),
        'positions': jnp.broadcast_to(jnp.arange(SEQ, dtype=jnp.int32), (BATCH, SEQ)),
        'ln_in_g': gain(ks[1], (D_MODEL,)),
        'ln_in_b': nrm(ks[2], (D_MODEL,), 0.02),
        'w_in': nrm(ks[3], (DEPTH, D_MODEL, IN_COLS), D_MODEL ** -0.5),
        'b_forget': 3.0 + nrm(ks[4], (DEPTH, FOX_HEADS), 0.1),
        'diff_lambda': nrm(ks[5], (DEPTH, 4, DIFF_DIM), 0.1),
        'diff_subln_g': gain(ks[6], (DEPTH, DIFF_V_DIM)),
        'mla_q_norm_g': gain(ks[7], (DEPTH, MLA_Q_RANK)),
        'mla_w_uq': nrm(ks[8], (DEPTH, MLA_Q_RANK, MLA_HEADS * (MLA_NOPE_DIM + MLA_ROPE_DIM)), MLA_Q_RANK ** -0.5),
        'mla_kv_norm_g': gain(ks[9], (DEPTH, MLA_KV_RANK)),
        'mla_w_ukv': nrm(ks[10], (DEPTH, MLA_KV_RANK, MLA_HEADS * (MLA_NOPE_DIM + MLA_V_DIM)), MLA_KV_RANK ** -0.5),
        'w_out': nrm(ks[11], (DEPTH, MIX_WIDTH, D_MODEL), MIX_WIDTH ** -0.5 * DEEPNORM_BETA),
        'ln1_g': gain(ks[12], (DEPTH, D_MODEL)),
        'ln1_b': nrm(ks[13], (DEPTH, D_MODEL), 0.02),
        'router_w': nrm(ks[14], (DEPTH, D_MODEL, N_EXPERTS), D_MODEL ** -0.5),
        'router_b': nrm(ks[15], (DEPTH, N_EXPERTS), 0.01),
        'w_gate_up': nrm(ks[16], (DEPTH, N_EXPERTS, D_MODEL, 2 * D_FF), D_MODEL ** -0.5),
        'b_gate_up': nrm(ks[17], (DEPTH, N_EXPERTS, 2 * D_FF), 0.01),
        'w_down': nrm(ks[18], (DEPTH, N_EXPERTS, D_FF, D_MODEL), D_FF ** -0.5 * DEEPNORM_BETA),
        'b_down': nrm(ks[19], (DEPTH, N_EXPERTS, D_MODEL), 0.01),
        'ln2_g': gain(ks[20], (DEPTH, D_MODEL)),
        'ln2_b': nrm(ks[21], (DEPTH, D_MODEL), 0.02),
    }


def reference(x, positions, ln_in_g, ln_in_b, w_in, b_forget, diff_lambda, diff_subln_g,
              mla_q_norm_g, mla_w_uq, mla_kv_norm_g, mla_w_ukv, w_out, ln1_g, ln1_b,
              router_w, router_b, w_gate_up, b_gate_up, w_down, b_down, ln2_g, ln2_b):
    h = _layer_norm(x, ln_in_g, ln_in_b)
    for l in range(DEPTH):
        lambda_init = 0.8 - 0.6 * math.exp(-0.3 * l)
        mix = _hybrid_mixer(h, positions, lambda_init, w_in[l], b_forget[l], diff_lambda[l],
                            diff_subln_g[l], mla_q_norm_g[l], mla_w_uq[l], mla_kv_norm_g[l],
                            mla_w_ukv[l], w_out[l])
        h = _layer_norm(DEEPNORM_ALPHA * h + mix, ln1_g[l], ln1_b[l])
        ffn = _moe(h, router_w[l], router_b[l], w_gate_up[l], b_gate_up[l], w_down[l], b_down[l])
        h = _layer_norm(DEEPNORM_ALPHA * h + ffn, ln2_g[l], ln2_b[l])
    return h
```

```python
import functools
import math

import jax
import jax.numpy as jnp
from jax import lax
from jax.experimental import pallas as pl
from jax.experimental.pallas import tpu as pltpu

F32 = jnp.float32
BF16 = jnp.bfloat16

SB_HEADS, SB_DIM = 4, 64
DIFF_HEADS, DIFF_DIM, DIFF_V_DIM = 4, 32, 64
FOX_HEADS, FOX_DIM = 4, 64
MLA_HEADS, MLA_Q_RANK, MLA_KV_RANK = 4, 256, 128
MLA_NOPE_DIM, MLA_ROPE_DIM, MLA_V_DIM = 64, 32, 64
TOP_K = 4
SWIGLU_LIMIT = 7.0
SWIGLU_ALPHA = 1.702
ROPE_THETA = 10000.0
LN_EPS = 1e-5
RMS_EPS = 1e-6
IN_SPLITS = (256, 256, 256, 256, 256, 256, 256, 256, 256, 4, 256, 128, 32)

LANES = 128
HEAD_PAIR = 128
VMEM_LIMIT = 56 * 1024 * 1024

TS_PROJ = 512
TQ = 256
TK = 256
TM_OUT = 512
TR_ROUTER = 512
MOE_BLOCK = 256
TD_DISPATCH = 256
TC_COMBINE = 256
NEG_BIG = -1e30


def _cparams(sem):
    return pltpu.CompilerParams(dimension_semantics=sem, vmem_limit_bytes=VMEM_LIMIT)


def _dot(a, b):
    return jnp.dot(a, b, preferred_element_type=F32)


def _dot_nt(a, b):
    return lax.dot_general(a, b, (((1,), (1,)), ((), ())), preferred_element_type=F32)


def _split3(x):
    a = x.astype(BF16)
    r = x - a.astype(F32)
    b = r.astype(BF16)
    c = (r - b.astype(F32)).astype(BF16)
    return a, b, c


def _tile_lanes(x, n):
    return jnp.concatenate([x] * (n // LANES), axis=1) if n > LANES else x


def _layer_norm_rows(y, g, b):
    mu = jnp.mean(y, axis=-1, keepdims=True)
    d = y - mu
    var = jnp.mean(d * d, axis=-1, keepdims=True)
    return d * lax.rsqrt(var + LN_EPS) * g + b


def _ln_kernel(x_ref, g_ref, b_ref, o_ref):
    o_ref[...] = _layer_norm_rows(x_ref[...], g_ref[...], b_ref[...])


def _layer_norm_call(x2d, g, b):
    n, d = x2d.shape
    tm = 512
    return pl.pallas_call(
        _ln_kernel,
        grid=(n // tm,),
        in_specs=[pl.BlockSpec((tm, d), lambda i: (i, 0)),
                  pl.BlockSpec((1, d), lambda i: (0, 0)),
                  pl.BlockSpec((1, d), lambda i: (0, 0))],
        out_specs=pl.BlockSpec((tm, d), lambda i: (i, 0)),
        out_shape=jax.ShapeDtypeStruct((n, d), F32),
        compiler_params=_cparams(("arbitrary",)),
        name="ln_in",
    )(x2d, g.reshape(1, d), b.reshape(1, d))


def _rope_table_kernel(pos_ref, invf_ref, cos_ref, sin_ref):
    s = pos_ref.shape[-1]
    pos = pos_ref[0].astype(F32)
    ang = _tile_lanes(invf_ref[...], s) * pos
    cos_ref[0] = jnp.cos(ang)
    sin_ref[0] = jnp.sin(ang)


def _rope_tables(positions):
    b, s = positions.shape
    half = DIFF_DIM // 2
    inv_freq = ROPE_THETA ** (-jnp.arange(0, DIFF_DIM, 2, dtype=F32) / DIFF_DIM)
    invf = jnp.broadcast_to(inv_freq[:, None], (half, LANES))
    out = jax.ShapeDtypeStruct((b, half, s), F32)
    return pl.pallas_call(
        _rope_table_kernel,
        grid=(b,),
        in_specs=[pl.BlockSpec((1, 1, s), lambda i: (i, 0, 0)),
                  pl.BlockSpec((half, LANES), lambda i: (0, 0))],
        out_specs=[pl.BlockSpec((1, half, s), lambda i: (i, 0, 0))] * 2,
        out_shape=[out, out],
        compiler_params=_cparams(("arbitrary",)),
        name="rope_tables",
    )(positions.reshape(b, 1, s), invf)


_R_SBQ, _R_SBV, _R_DFQ, _R_DFK, _R_DFV, _R_FXQ, _R_FXV, _R_CQ, _R_CKV, _R_KR, _R_END = (
    0, 256, 512, 768, 1024, 1280, 1536, 1792, 2048, 2176, 2208)


def _rope_pair(x1, x2, cos, sin):
    return x1 * cos - x2 * sin, x2 * cos + x1 * sin


def _proj_kernel(h_ref, wt_ref, wn_ref, wf_ref, bf_ref, gq_ref, gkv_ref, wuq_ref, wukv_ref,
                 cos_ref, sin_ref,
                 sbq_ref, sbk_ref, sbv_ref, dfq_ref, dfk_ref, dfv_ref,
                 fxq_ref, fxk_ref, fxv_ref, fb_ref, mq_ref, mk_ref, mv_ref,
                 carry_ref):
    ts = h_ref.shape[1]
    nkb = ts // TK
    hb = h_ref[0].astype(BF16)
    cos = cos_ref[0]
    sin = sin_ref[0]

    def nt(lo, hi):
        return _dot_nt(wt_ref[lo:hi, :], hb)

    def store_vt(ref, x):
        for j in range(nkb):
            ref[0, j] = x[:, j * TK:(j + 1) * TK].astype(BF16)

    sbq_ref[0] = nt(_R_SBQ, _R_SBV).astype(BF16)
    store_vt(sbv_ref, nt(_R_SBV, _R_DFQ))
    fxq_ref[0] = nt(_R_FXQ, _R_FXV).astype(BF16)
    store_vt(fxv_ref, nt(_R_FXV, _R_CQ))
    kn = _dot(hb, wn_ref[...])
    sbk_ref[0] = kn[:, :256].astype(BF16)
    fxk_ref[0] = kn[:, 256:].astype(BF16)

    store_vt(dfv_ref, nt(_R_DFV, _R_FXQ))
    xq = nt(_R_DFQ, _R_DFK) * (1.0 / math.sqrt(DIFF_DIM))
    xk = nt(_R_DFK, _R_DFV)
    k_pieces = []
    for j in range(2 * DIFF_HEADS):
        lo = j * DIFF_DIM
        q1, q2 = _rope_pair(xq[lo:lo + 16], xq[lo + 16:lo + 32], cos, sin)
        dfq_ref[0, lo:lo + 16, :] = q1.astype(BF16)
        dfq_ref[0, lo + 16:lo + 32, :] = q2.astype(BF16)
        k1, k2 = _rope_pair(xk[lo:lo + 16], xk[lo + 16:lo + 32], cos, sin)
        k_pieces += [k1, k2]
    for c in range(2):
        kt = jnp.concatenate(k_pieces[8 * c:8 * c + 8], axis=0)
        dfk_ref[0, :, c * HEAD_PAIR:(c + 1) * HEAD_PAIR] = kt.T.astype(BF16)

    @pl.when(pl.program_id(1) == 0)
    def _():
        carry_ref[...] = jnp.zeros_like(carry_ref)

    f = _dot(hb, wf_ref[...]) + bf_ref[...]
    logf = jnp.minimum(f, 0.0) - jnp.log(1.0 + jnp.exp(-jnp.abs(f)))
    row = lax.broadcasted_iota(jnp.int32, (ts, ts), 0)
    col = lax.broadcasted_iota(jnp.int32, (ts, ts), 1)
    tri = (col <= row).astype(BF16)
    p0, p1, p2 = _split3(logf)
    cum = _dot(tri, p0) + _dot(tri, p1) + _dot(tri, p2) + carry_ref[0:1, :]
    for hh in range(FOX_HEADS):
        fb_ref[0, hh] = cum[:, hh * LANES:(hh + 1) * LANES]
    carry_ref[0:1, :] = cum[ts - 1:ts, :]

    cq = nt(_R_CQ, _R_CKV)
    cqn = cq * lax.rsqrt(jnp.mean(cq * cq, axis=0, keepdims=True) + RMS_EPS) * _tile_lanes(gq_ref[...], ts)
    qd = _dot(wuq_ref[...], cqn.astype(BF16)) * (1.0 / math.sqrt(MLA_NOPE_DIM + MLA_ROPE_DIM))
    for hh in range(MLA_HEADS):
        base = hh * HEAD_PAIR
        mq_ref[0, base:base + 64, :] = qd[base:base + 64].astype(BF16)
        r1, r2 = _rope_pair(qd[base + 64:base + 80], qd[base + 80:base + 96], cos, sin)
        mq_ref[0, base + 64:base + 80, :] = r1.astype(BF16)
        mq_ref[0, base + 80:base + 96, :] = r2.astype(BF16)
        mq_ref[0, base + 96:base + 128, :] = qd[base + 96:base + 128].astype(BF16)
    ckv = nt(_R_CKV, _R_KR)
    ckvn = ckv * lax.rsqrt(jnp.mean(ckv * ckv, axis=0, keepdims=True) + RMS_EPS) * _tile_lanes(gkv_ref[...], ts)
    kvd = _dot(wukv_ref[...], ckvn.astype(BF16))
    store_vt(mv_ref, kvd[256:512])
    kr = nt(_R_KR, _R_END)
    kr1, kr2 = _rope_pair(kr[0:16], kr[16:32], cos, sin)
    zpad = jnp.zeros((32, ts), F32)
    for hh in range(MLA_HEADS):
        kt = jnp.concatenate([kvd[hh * 64:(hh + 1) * 64], kr1, kr2, zpad], axis=0)
        mk_ref[0, :, hh * HEAD_PAIR:(hh + 1) * HEAD_PAIR] = kt.T.astype(BF16)


def _prep_proj_weights(w_in, b_forget, q_norm_g, w_uq, kv_norm_g, w_ukv):
    offs = [0]
    for c in IN_SPLITS:
        offs.append(offs[-1] + c)
    (sb_q, sb_k, sb_v, df_q, df_k, df_v, fx_q, fx_k, fx_v, fx_f, m_cq, m_ckv, m_kr) = [
        w_in[:, offs[i]:offs[i + 1]] for i in range(len(IN_SPLITS))]
    sb_scale = 1.0 / math.sqrt(SB_DIM)
    wt = jnp.concatenate([sb_q * sb_scale, sb_v, df_q, df_k, df_v, fx_q * sb_scale, fx_v,
                          m_cq, m_ckv, m_kr], axis=1).T.astype(BF16)
    wn = jnp.concatenate([sb_k, fx_k], axis=1).astype(BF16)
    wf = jnp.repeat(fx_f, LANES, axis=1).astype(BF16)
    bf = jnp.repeat(b_forget, LANES)[None, :].astype(F32)
    gq = jnp.broadcast_to(q_norm_g[:, None], (MLA_Q_RANK, LANES)).astype(F32)
    gkv = jnp.broadcast_to(kv_norm_g[:, None], (MLA_KV_RANK, LANES)).astype(F32)
    dq = MLA_NOPE_DIM + MLA_ROPE_DIM
    wuq = w_uq.reshape(MLA_Q_RANK, MLA_HEADS, dq)
    wuq = jnp.pad(wuq, ((0, 0), (0, 0), (0, HEAD_PAIR - dq)))
    wuq = wuq.reshape(MLA_Q_RANK, MLA_HEADS * HEAD_PAIR).T.astype(BF16)
    wukv = w_ukv.reshape(MLA_KV_RANK, MLA_HEADS, MLA_NOPE_DIM + MLA_V_DIM)
    wukv = jnp.concatenate([wukv[:, :, :MLA_NOPE_DIM].reshape(MLA_KV_RANK, -1),
                            wukv[:, :, MLA_NOPE_DIM:].reshape(MLA_KV_RANK, -1)], axis=1).T.astype(BF16)
    return wt, wn, wf, bf, gq, gkv, wuq, wukv


def _proj_call(h3, pw, cos_t, sin_t):
    b, s, d = h3.shape
    ts = TS_PROJ
    wt, wn, wf, bf, gq, gkv, wuq, wukv = pw
    nkb = s // TK

    def const(a):
        return pl.BlockSpec(a.shape, lambda i, j: (0,) * a.ndim)

    def fmaj(c):
        return (jax.ShapeDtypeStruct((b, c, s), BF16), pl.BlockSpec((1, c, ts), lambda i, j: (i, 0, j)))

    def tmaj(c):
        return (jax.ShapeDtypeStruct((b, s, c), BF16), pl.BlockSpec((1, ts, c), lambda i, j: (i, j, 0)))

    def vblk(c):
        return (jax.ShapeDtypeStruct((b, nkb, c, TK), BF16),
                pl.BlockSpec((1, ts // TK, c, TK), lambda i, j: (i, j, 0, 0)))

    fb = (jax.ShapeDtypeStruct((b, FOX_HEADS, s, LANES), F32),
          pl.BlockSpec((1, FOX_HEADS, ts, LANES), lambda i, j: (i, 0, j, 0)))
    outs = [fmaj(256), tmaj(256), vblk(256),
            fmaj(256), tmaj(256), vblk(256),
            fmaj(256), tmaj(256), vblk(256), fb,
            fmaj(512), tmaj(512), vblk(256)]
    return pl.pallas_call(
        _proj_kernel,
        grid=(b, s // ts),
        in_specs=[pl.BlockSpec((1, ts, d), lambda i, j: (i, j, 0)),
                  const(wt), const(wn), const(wf), const(bf), const(gq), const(gkv), const(wuq), const(wukv),
                  pl.BlockSpec((1, 16, ts), lambda i, j: (i, 0, j)),
                  pl.BlockSpec((1, 16, ts), lambda i, j: (i, 0, j))],
        out_specs=[o[1] for o in outs],
        out_shape=[o[0] for o in outs],
        scratch_shapes=[pltpu.VMEM((8, FOX_HEADS * LANES), F32)],
        compiler_params=_cparams(("arbitrary", "arbitrary")),
        name="in_proj",
    )(h3, wt, wn, wf, bf, gq, gkv, wuq, wukv, cos_t, sin_t)


def _q_variants(q, subs):
    out = []
    rows = lax.broadcasted_iota(jnp.int32, (HEAD_PAIR, q.shape[1]), 0)
    for (qlo, mlo, mhi, _, _) in subs:
        qa = q[qlo:qlo + HEAD_PAIR]
        if mhi - mlo < HEAD_PAIR:
            qa = jnp.where((rows >= mlo) & (rows < mhi), qa, jnp.zeros_like(qa))
        out.append(qa)
    return out


def _store_pair(o_ref, o0, o1):
    o_ref[0] = jnp.concatenate([o0, o1], axis=0).T.astype(o_ref.dtype)


def _softmax_attn_kernel(mode, subs, lambda_init, *refs):
    if mode == "fox":
        q_ref, k_ref, v_ref, fb_ref, o_ref, m_ref, l_ref, acc_ref = refs
    elif mode == "diff":
        q_ref, k_ref, v_ref, dl_ref, g_ref, o_ref, m_ref, l_ref, acc_ref = refs
    else:
        q_ref, k_ref, v_ref, o_ref, m_ref, l_ref, acc_ref = refs
    qi = pl.program_id(2)
    qs = _q_variants(q_ref[0], subs)
    m_ref[...] = jnp.full(m_ref.shape, NEG_BIG, F32)
    l_ref[...] = jnp.zeros_like(l_ref)
    acc_ref[...] = jnp.zeros_like(acc_ref)

    def step(kj, diagonal):
        koff = pl.multiple_of(kj * TK, TK)
        kb = k_ref[0, pl.ds(koff, TK), :]
        vb = v_ref[0, kj]
        for a, (_, _, _, klo, vlo) in enumerate(subs):
            st = _dot(kb[:, klo:klo + HEAD_PAIR], qs[a])
            if mode == "fox":
                st = st - _tile_lanes(fb_ref[0, a, pl.ds(koff, TK), :], TQ)
            if diagonal:
                r = lax.broadcasted_iota(jnp.int32, (TK, TQ), 0)
                c = lax.broadcasted_iota(jnp.int32, (TK, TQ), 1)
                st = jnp.where(r <= c, st, NEG_BIG)
            m_old = m_ref[a]
            m_new = jnp.maximum(m_old, jnp.max(st, axis=0, keepdims=True))
            alpha = jnp.exp(m_old - m_new)
            p = jnp.exp(st - m_new)
            l_ref[a] = alpha * l_ref[a] + jnp.sum(p, axis=0, keepdims=True)
            acc_ref[a] = alpha * acc_ref[a] + _dot(vb[vlo:vlo + 64], p.astype(BF16))
            m_ref[a] = m_new

    def body(kj, carry):
        step(kj, False)
        return carry

    lax.fori_loop(0, qi, body, 0)
    step(qi, True)

    outs = [acc_ref[a] / l_ref[a] for a in range(len(subs))]
    if mode == "diff":
        dl = dl_ref[0]
        lam = (jnp.exp(jnp.sum(dl[0:1] * dl[1:2], axis=1, keepdims=True))
               - jnp.exp(jnp.sum(dl[2:3] * dl[3:4], axis=1, keepdims=True)) + lambda_init)
        g = _tile_lanes(g_ref[...], TQ)
        res = []
        for hh in range(2):
            d = outs[2 * hh] - lam * outs[2 * hh + 1]
            y = d * lax.rsqrt(jnp.mean(d * d, axis=0, keepdims=True) + RMS_EPS) * g
            res.append(y * (1.0 - lambda_init))
        _store_pair(o_ref, res[0], res[1])
    else:
        _store_pair(o_ref, outs[0], outs[1])


def _sb_attn_kernel(subs, q_ref, k_ref, v_ref, o_ref, c_ref, acc_ref):
    qi = pl.program_id(2)
    qs = _q_variants(q_ref[0], subs)
    c_ref[...] = jnp.zeros_like(c_ref)
    acc_ref[...] = jnp.zeros_like(acc_ref)
    r = lax.broadcasted_iota(jnp.int32, (TK, TQ), 0)
    c = lax.broadcasted_iota(jnp.int32, (TK, TQ), 1)
    later = (c > r).astype(BF16)

    def step(kj, diagonal):
        koff = pl.multiple_of(kj * TK, TK)
        kb = k_ref[0, pl.ds(koff, TK), :]
        vb = v_ref[0, kj]
        for a, (_, _, _, klo, vlo) in enumerate(subs):
            z = _dot(kb[:, klo:klo + HEAD_PAIR], qs[a])
            lk = -(jnp.maximum(z, 0.0) + jnp.log(1.0 + jnp.exp(-jnp.abs(z))))
            if diagonal:
                lk = jnp.where(r < c, lk, 0.0)
            after = _dot(later, lk.astype(BF16))
            w = jnp.exp(z + lk + after + c_ref[a])
            if diagonal:
                w = jnp.where(r < c, w, 0.0)
            acc_ref[a] = acc_ref[a] + _dot(vb[vlo:vlo + 64], w.astype(BF16))
            c_ref[a] = c_ref[a] + jnp.sum(lk, axis=0, keepdims=True)

    step(qi, True)

    def body(i, carry):
        step(qi - 1 - i, False)
        return carry

    lax.fori_loop(0, qi, body, 0)
    _store_pair(o_ref, acc_ref[0], acc_ref[1])


def _attn_call(mode, lambda_init, q, k, v, extra=()):
    b, cq, s = q.shape
    ck = k.shape[2]
    npair = 2
    qb, kb = cq // npair, ck // npair
    if mode == "mla":
        subs = ((0, 0, 128, 0, 0), (128, 0, 128, 128, 64))
    elif mode == "diff":
        subs = ((0, 0, 32, 0, 0), (0, 32, 64, 0, 0), (0, 64, 96, 0, 64), (0, 96, 128, 0, 64))
    else:
        subs = ((0, 0, 64, 0, 0), (0, 64, 128, 0, 64))
    nsub = len(subs)
    in_specs = [pl.BlockSpec((1, qb, TQ), lambda i, p, j: (i, p, j)),
                pl.BlockSpec((1, s, kb), lambda i, p, j: (i, 0, p)),
                pl.BlockSpec((1, s // TK, HEAD_PAIR, TK), lambda i, p, j: (i, 0, p, 0))]
    args = [q, k, v]
    scratch = [pltpu.VMEM((nsub, 1, TQ), F32), pltpu.VMEM((nsub, 1, TQ), F32), pltpu.VMEM((nsub, 64, TQ), F32)]
    if mode == "sb":
        kern = functools.partial(_sb_attn_kernel, subs)
        scratch = [pltpu.VMEM((nsub, 1, TQ), F32), pltpu.VMEM((nsub, 64, TQ), F32)]
    else:
        kern = functools.partial(_softmax_attn_kernel, mode, subs, lambda_init)
        if mode == "fox":
            (fb,) = extra
            in_specs.append(pl.BlockSpec((1, 2, s, LANES), lambda i, p, j: (i, p, 0, 0)))
            args.append(fb)
        elif mode == "diff":
            dl, g = extra
            in_specs += [pl.BlockSpec((1, 4, DIFF_DIM), lambda i, p, j: (0, 0, 0)),
                         pl.BlockSpec((DIFF_V_DIM, LANES), lambda i, p, j: (0, 0))]
            args += [dl, g]
    return pl.pallas_call(
        kern,
        grid=(b, npair, s // TQ),
        in_specs=in_specs,
        out_specs=pl.BlockSpec((1, TQ, HEAD_PAIR), lambda i, p, j: (i, j, p)),
        out_shape=jax.ShapeDtypeStruct((b, s, npair * HEAD_PAIR), BF16),
        scratch_shapes=scratch,
        compiler_params=_cparams(("arbitrary", "arbitrary", "arbitrary")),
        name="attn_" + mode,
    )(*args)


def _out_kernel(alpha, ma_ref, mb_ref, mc_ref, md_ref, w_ref, h_ref, g_ref, b_ref, o_ref):
    mix = _dot(ma_ref[...], w_ref[0:256, :])
    mix += _dot(mb_ref[...], w_ref[256:512, :])
    mix += _dot(mc_ref[...], w_ref[512:768, :])
    mix += _dot(md_ref[...], w_ref[768:1024, :])
    o_ref[...] = _layer_norm_rows(alpha * h_ref[...] + mix, g_ref[...], b_ref[...])


def _out_call(alpha, mixes, w_out_bf, h2d, g, b):
    n, d = h2d.shape
    tm = TM_OUT
    mspec = pl.BlockSpec((tm, 256), lambda i: (i, 0))
    return pl.pallas_call(
        functools.partial(_out_kernel, alpha),
        grid=(n // tm,),
        in_specs=[mspec, mspec, mspec, mspec,
                  pl.BlockSpec(w_out_bf.shape, lambda i: (0, 0)),
                  pl.BlockSpec((tm, d), lambda i: (i, 0)),
                  pl.BlockSpec((1, d), lambda i: (0, 0)),
                  pl.BlockSpec((1, d), lambda i: (0, 0))],
        out_specs=pl.BlockSpec((tm, d), lambda i: (i, 0)),
        out_shape=jax.ShapeDtypeStruct((n, d), F32),
        compiler_params=_cparams(("arbitrary",)),
        name="out_proj_ln",
    )(*mixes, w_out_bf, h2d, g.reshape(1, d), b.reshape(1, d))


def _router_kernel(h_ref, w_ref, b_ref, exp_ref, gate_ref, rank_ref, cnt_ref, carry_ref):
    tr = h_ref.shape[0]
    ne = w_ref.shape[0]

    @pl.when(pl.program_id(0) == 0)
    def _():
        carry_ref[...] = jnp.zeros_like(carry_ref)

    h0, h1, h2 = _split3(h_ref[...])
    w0, w1, w2 = _split3(w_ref[...])
    logits = (_dot_nt(w0, h0) + _dot_nt(w0, h1) + _dot_nt(w1, h0)
              + _dot_nt(w1, h1) + _dot_nt(w0, h2) + _dot_nt(w2, h0))
    logits = logits + _tile_lanes(b_ref[...], tr)
    eid = lax.broadcasted_iota(jnp.int32, (ne, tr), 0).astype(F32)
    vals, sels = [], []
    cur = logits
    for k in range(TOP_K):
        m = jnp.max(cur, axis=0, keepdims=True)
        idx = jnp.min(jnp.where(cur == m, eid, float(ne)), axis=0, keepdims=True)
        sel = eid == idx
        cur = jnp.where(sel, -jnp.inf, cur)
        vals.append(m)
        sels.append(sel)
        exp_ref[k:k + 1, :] = idx.astype(jnp.int32)
    es = [jnp.exp(v - vals[0]) for v in vals]
    den = es[0] + es[1] + es[2] + es[3]
    for k in range(TOP_K):
        gate_ref[k:k + 1, :] = es[k] / den
    exp_ref[TOP_K:, :] = jnp.zeros((8 - TOP_K, tr), jnp.int32)
    gate_ref[TOP_K:, :] = jnp.zeros((8 - TOP_K, tr), F32)
    rank_ref[TOP_K:, :] = jnp.zeros((8 - TOP_K, tr), jnp.int32)

    onehot = (sels[0] | sels[1] | sels[2] | sels[3]).astype(BF16)
    rj = lax.broadcasted_iota(jnp.int32, (tr, tr), 0)
    ct = lax.broadcasted_iota(jnp.int32, (tr, tr), 1)
    before = (rj < ct).astype(BF16)
    cum = _dot(onehot, before) + _tile_lanes(carry_ref[...], tr)
    for k in range(TOP_K):
        rk = jnp.sum(jnp.where(sels[k], cum, 0.0), axis=0, keepdims=True)
        rank_ref[k:k + 1, :] = rk.astype(jnp.int32)
    total = carry_ref[...] + _dot(onehot, jnp.ones((tr, LANES), BF16))
    carry_ref[...] = total
    cnt_ref[...] = total


def _router_call(h2d, router_w, router_b):
    n, d = h2d.shape
    ne = router_w.shape[1]
    tr = TR_ROUTER
    wt = router_w.T
    bb = jnp.broadcast_to(router_b[:, None], (ne, LANES))
    rows = jax.ShapeDtypeStruct((8, n), jnp.int32)
    return pl.pallas_call(
        _router_kernel,
        grid=(n // tr,),
        in_specs=[pl.BlockSpec((tr, d), lambda i: (i, 0)),
                  pl.BlockSpec((ne, d), lambda i: (0, 0)),
                  pl.BlockSpec((ne, LANES), lambda i: (0, 0))],
        out_specs=[pl.BlockSpec((8, tr), lambda i: (0, i)),
                   pl.BlockSpec((8, tr), lambda i: (0, i)),
                   pl.BlockSpec((8, tr), lambda i: (0, i)),
                   pl.BlockSpec((ne, LANES), lambda i: (0, 0))],
        out_shape=[rows, jax.ShapeDtypeStruct((8, n), F32), rows,
                   jax.ShapeDtypeStruct((ne, LANES), F32)],
        scratch_shapes=[pltpu.VMEM((ne, LANES), F32)],
        compiler_params=_cparams(("arbitrary",)),
        name="router",
    )(h2d, wt, bb)


def _row_copy(src_hbm, src_row, dst, dst_row, sem):
    return pltpu.make_async_copy(src_hbm.at[pl.ds(src_row, 1)], dst.at[pl.ds(dst_row, 1)], sem)


def _dispatch_kernel(slot_hbm, h_hbm, xs_in_hbm, xs_hbm, idx_ref, sem_idx, sem):
    del xs_in_hbm
    td = TD_DISPATCH
    base = pl.program_id(0) * td
    cp = pltpu.make_async_copy(slot_hbm.at[pl.ds(base * TOP_K, td * TOP_K)], idx_ref, sem_idx)
    cp.start()
    cp.wait()

    def issue(j, carry):
        for k in range(TOP_K):
            _row_copy(h_hbm, base + j, xs_hbm, idx_ref[j * TOP_K + k], sem).start()
        return carry

    lax.fori_loop(0, td, issue, 0)

    def drain(j, carry):
        for k in range(TOP_K):
            _row_copy(h_hbm, base + j, xs_hbm, idx_ref[j * TOP_K + k], sem).wait()
        return carry

    lax.fori_loop(0, td, drain, 0)


def _dispatch_call(slot_flat, h2d, n_slots):
    n, d = h2d.shape
    td = TD_DISPATCH
    xs0 = jnp.zeros((n_slots, d), F32)
    anyspec = pl.BlockSpec(memory_space=pl.ANY)
    return pl.pallas_call(
        _dispatch_kernel,
        grid=(n // td,),
        in_specs=[anyspec, anyspec, anyspec],
        out_specs=anyspec,
        out_shape=jax.ShapeDtypeStruct((n_slots, d), F32),
        scratch_shapes=[pltpu.SMEM((td * TOP_K,), jnp.int32),
                        pltpu.SemaphoreType.DMA(()), pltpu.SemaphoreType.DMA(())],
        input_output_aliases={2: 0},
        compiler_params=_cparams(("arbitrary",)),
        name="moe_dispatch",
    )(slot_flat, h2d, xs0)


def _expert_kernel(bexp_ref, nused_ref, x_ref, wgu_ref, bgu_ref, wd_ref, bd_ref, y_ref, wgu_s, wd_s):
    i = pl.program_id(0)
    dff = wd_ref.shape[2]

    @pl.when(i < nused_ref[0])
    def _():
        e = bexp_ref[i]
        prev = bexp_ref[jnp.maximum(i - 1, 0)]

        @pl.when((i == 0) | (e != prev))
        def _():
            wgu_s[...] = wgu_ref[0, 0].astype(BF16)
            wd_s[...] = wd_ref[0, 0].astype(BF16)

        xb = x_ref[...].astype(BF16)
        gu = _dot(xb, wgu_s[...]) + bgu_ref[0]
        glu = jnp.minimum(gu[:, :dff], SWIGLU_LIMIT)
        lin = jnp.clip(gu[:, dff:], -SWIGLU_LIMIT, SWIGLU_LIMIT)
        act = glu * (1.0 / (1.0 + jnp.exp(-SWIGLU_ALPHA * glu))) * (lin + 1.0)
        y_ref[...] = _dot(act.astype(BF16), wd_s[...]) + bd_ref[0]

    @pl.when(i >= nused_ref[0])
    def _():
        y_ref[...] = jnp.zeros_like(y_ref)


def _expert_call(layer, block_exp, n_used, xs, w_gate_up, b_gate_up, w_down, b_down):
    n_slots, d = xs.shape
    _, ne, _, dff2 = w_gate_up.shape
    dff = dff2 // 2
    nb = n_slots // MOE_BLOCK

    def blk(i, bexp, nused):
        return jnp.minimum(i, nused[0] - 1)

    def wmap(i, bexp, nused):
        return (layer, bexp[blk(i, bexp, nused)], 0, 0)

    def bmap(i, bexp, nused):
        return (layer * ne + bexp[blk(i, bexp, nused)], 0, 0)

    grid_spec = pltpu.PrefetchScalarGridSpec(
        num_scalar_prefetch=2,
        grid=(nb,),
        in_specs=[pl.BlockSpec((MOE_BLOCK, d), lambda i, bexp, nused: (blk(i, bexp, nused), 0)),
                  pl.BlockSpec((1, 1, d, dff2), wmap),
                  pl.BlockSpec((1, 1, dff2), bmap),
                  pl.BlockSpec((1, 1, dff, d), wmap),
                  pl.BlockSpec((1, 1, d), bmap)],
        out_specs=pl.BlockSpec((MOE_BLOCK, d), lambda i, bexp, nused: (i, 0)),
        scratch_shapes=[pltpu.VMEM((d, dff2), BF16), pltpu.VMEM((dff, d), BF16)],
    )
    return pl.pallas_call(
        _expert_kernel,
        grid_spec=grid_spec,
        out_shape=jax.ShapeDtypeStruct((n_slots, d), F32),
        compiler_params=_cparams(("arbitrary",)),
        name="moe_experts",
    )(block_exp, n_used, xs, w_gate_up, b_gate_up.reshape(-1, 1, dff2), w_down, b_down.reshape(-1, 1, d))


def _combine_kernel(alpha, slot_hbm, ys_hbm, gate_ref, h_ref, g_ref, b_ref, o_ref, idx_ref, buf_ref, sem_idx, sem):
    tc = TC_COMBINE
    base = pl.program_id(0) * tc
    cp = pltpu.make_async_copy(slot_hbm.at[pl.ds(base * TOP_K, tc * TOP_K)], idx_ref, sem_idx)
    cp.start()
    cp.wait()

    def issue(j, carry):
        for k in range(TOP_K):
            _row_copy(ys_hbm, idx_ref[j * TOP_K + k], buf_ref.at[k], j, sem).start()
        return carry

    lax.fori_loop(0, tc, issue, 0)

    def drain(j, carry):
        for k in range(TOP_K):
            _row_copy(ys_hbm, idx_ref[j * TOP_K + k], buf_ref.at[k], j, sem).wait()
        return carry

    lax.fori_loop(0, tc, drain, 0)

    gate = gate_ref[...]
    y = alpha * h_ref[...]
    for k in range(TOP_K):
        y = y + buf_ref[k] * gate[:, k:k + 1]
    o_ref[...] = _layer_norm_rows(y, g_ref[...], b_ref[...])


def _combine_call(alpha, slot_flat, ys, gate_tok, h2d, g, b):
    n, d = h2d.shape
    tc = TC_COMBINE
    anyspec = pl.BlockSpec(memory_space=pl.ANY)
    return pl.pallas_call(
        functools.partial(_combine_kernel, alpha),
        grid=(n // tc,),
        in_specs=[anyspec, anyspec,
                  pl.BlockSpec((tc, TOP_K), lambda i: (i, 0)),
                  pl.BlockSpec((tc, d), lambda i: (i, 0)),
                  pl.BlockSpec((1, d), lambda i: (0, 0)),
                  pl.BlockSpec((1, d), lambda i: (0, 0))],
        out_specs=pl.BlockSpec((tc, d), lambda i: (i, 0)),
        out_shape=jax.ShapeDtypeStruct((n, d), F32),
        scratch_shapes=[pltpu.SMEM((tc * TOP_K,), jnp.int32),
                        pltpu.VMEM((TOP_K, tc, d), F32),
                        pltpu.SemaphoreType.DMA(()), pltpu.SemaphoreType.DMA(())],
        compiler_params=_cparams(("arbitrary",)),
        name="moe_combine_ln",
    )(slot_flat, ys, gate_tok, h2d, g.reshape(1, d), b.reshape(1, d))


def _moe_layer(layer, alpha, h2d, router_w, router_b, w_gate_up, b_gate_up, w_down, b_down, g, b):
    n, d = h2d.shape
    ne = router_w.shape[1]
    exp_t, gate_t, rank_t, cnt = _router_call(h2d, router_w, router_b)
    counts = cnt[:, 0].astype(jnp.int32)
    padded = (counts + MOE_BLOCK - 1) // MOE_BLOCK * MOE_BLOCK
    pend = jnp.cumsum(padded)
    pstart = pend - padded
    onehot = exp_t[:TOP_K, :, None] == jnp.arange(ne, dtype=jnp.int32)[None, None, :]
    slot_t = rank_t[:TOP_K] + jnp.sum(jnp.where(onehot, pstart[None, None, :], 0), axis=-1)
    slot_flat = slot_t.T.reshape(-1)
    nb = n * TOP_K // MOE_BLOCK + ne
    n_used = (pend[-1] // MOE_BLOCK).astype(jnp.int32).reshape(1)
    bstart = jnp.arange(nb, dtype=jnp.int32) * MOE_BLOCK
    block_exp = jnp.minimum(jnp.sum(pend[None, :] <= bstart[:, None], axis=1), ne - 1).astype(jnp.int32)

    xs = _dispatch_call(slot_flat, h2d, nb * MOE_BLOCK)
    ys = _expert_call(layer, block_exp, n_used, xs, w_gate_up, b_gate_up, w_down, b_down)
    return _combine_call(alpha, slot_flat, ys, gate_t[:TOP_K].T, h2d, g, b)


def kernel(x, positions, ln_in_g, ln_in_b, w_in, b_forget, diff_lambda, diff_subln_g, mla_q_norm_g, mla_w_uq, mla_kv_norm_g, mla_w_ukv, w_out, ln1_g, ln1_b, router_w, router_b, w_gate_up, b_gate_up, w_down, b_down, ln2_g, ln2_b):
    bsz, s, d = x.shape
    depth = w_in.shape[0]
    n = bsz * s
    alpha = (2 * depth) ** 0.25
    cos_t, sin_t = _rope_tables(positions)
    h = _layer_norm_call(x.reshape(n, d), ln_in_g, ln_in_b)
    g_sub = jnp.broadcast_to(diff_subln_g[:, :, None], (depth, DIFF_V_DIM, LANES))
    for l in range(depth):
        lambda_init = 0.8 - 0.6 * math.exp(-0.3 * l)
        pw = _prep_proj_weights(w_in[l], b_forget[l], mla_q_norm_g[l], mla_w_uq[l], mla_kv_norm_g[l], mla_w_ukv[l])
        (sbq, sbk, sbv, dfq, dfk, dfv, fxq, fxk, fxv, fb, mq, mk, mv) = _proj_call(h.reshape(bsz, s, d), pw, cos_t, sin_t)
        mix_a = _attn_call("sb", lambda_init, sbq, sbk, sbv)
        mix_b = _attn_call("diff", lambda_init, dfq, dfk, dfv, (diff_lambda[l:l + 1], g_sub[l]))
        mix_c = _attn_call("fox", lambda_init, fxq, fxk, fxv, (fb,))
        mix_d = _attn_call("mla", lambda_init, mq, mk, mv)
        mixes = [m.reshape(n, 256) for m in (mix_a, mix_b, mix_c, mix_d)]
        h = _out_call(alpha, mixes, w_out[l].astype(BF16), h, ln1_g[l], ln1_b[l])
        h = _moe_layer(l, alpha, h, router_w[l], router_b[l], w_gate_up, b_gate_up, w_down, b_down, ln2_g[l], ln2_b[l])
    return h.reshape(bsz, s, d)
```

```python
import functools
import math

import jax
import jax.numpy as jnp
from jax import lax
from jax.experimental import pallas as pl
from jax.experimental.pallas import tpu as pltpu

F32 = jnp.float32
BF16 = jnp.bfloat16

SB_HEADS, SB_DIM = 4, 64
DIFF_HEADS, DIFF_DIM, DIFF_V_DIM = 4, 32, 64
FOX_HEADS, FOX_DIM = 4, 64
MLA_HEADS, MLA_Q_RANK, MLA_KV_RANK = 4, 256, 128
MLA_NOPE_DIM, MLA_ROPE_DIM, MLA_V_DIM = 64, 32, 64
TOP_K = 4
SWIGLU_LIMIT = 7.0
SWIGLU_ALPHA = 1.702
ROPE_THETA = 10000.0
LN_EPS = 1e-5
RMS_EPS = 1e-6
IN_SPLITS = (256, 256, 256, 256, 256, 256, 256, 256, 256, 4, 256, 128, 32)

LANES = 128
HEAD_PAIR = 128
VMEM_LIMIT = 56 * 1024 * 1024

TS_PROJ = 512
TQ = 256
TK = 256
TM_OUT = 512
TR_ROUTER = 512
MOE_BLOCK = 256
TD_DISPATCH = 256
TC_COMBINE = 256
NEG_BIG = -1e30
SB_UNDERFLOW = 104.0


def _cparams(sem):
    return pltpu.CompilerParams(dimension_semantics=sem, vmem_limit_bytes=VMEM_LIMIT)


def _dot(a, b):
    return jnp.dot(a, b, preferred_element_type=F32)


def _dot_nt(a, b):
    return lax.dot_general(a, b, (((1,), (1,)), ((), ())), preferred_element_type=F32)


def _split3(x):
    a = x.astype(BF16)
    r = x - a.astype(F32)
    b = r.astype(BF16)
    c = (r - b.astype(F32)).astype(BF16)
    return a, b, c


def _tile_lanes(x, n):
    return jnp.concatenate([x] * (n // LANES), axis=1) if n > LANES else x


def _layer_norm_rows(y, g, b):
    mu = jnp.mean(y, axis=-1, keepdims=True)
    d = y - mu
    var = jnp.mean(d * d, axis=-1, keepdims=True)
    return d * lax.rsqrt(var + LN_EPS) * g + b


def _ln_kernel(x_ref, g_ref, b_ref, o_ref):
    o_ref[...] = _layer_norm_rows(x_ref[...], g_ref[...], b_ref[...])


def _layer_norm_call(x2d, g, b):
    n, d = x2d.shape
    tm = 512
    return pl.pallas_call(
        _ln_kernel,
        grid=(n // tm,),
        in_specs=[pl.BlockSpec((tm, d), lambda i: (i, 0)),
                  pl.BlockSpec((1, d), lambda i: (0, 0)),
                  pl.BlockSpec((1, d), lambda i: (0, 0))],
        out_specs=pl.BlockSpec((tm, d), lambda i: (i, 0)),
        out_shape=jax.ShapeDtypeStruct((n, d), F32),
        compiler_params=_cparams(("arbitrary",)),
        name="ln_in",
    )(x2d, g.reshape(1, d), b.reshape(1, d))


def _rope_table_kernel(pos_ref, invf_ref, cos_ref, sin_ref):
    s = pos_ref.shape[-1]
    pos = pos_ref[0].astype(F32)
    ang = _tile_lanes(invf_ref[...], s) * pos
    cos_ref[0] = jnp.cos(ang)
    sin_ref[0] = jnp.sin(ang)


def _rope_tables(positions):
    b, s = positions.shape
    half = DIFF_DIM // 2
    inv_freq = ROPE_THETA ** (-jnp.arange(0, DIFF_DIM, 2, dtype=F32) / DIFF_DIM)
    invf = jnp.broadcast_to(inv_freq[:, None], (half, LANES))
    out = jax.ShapeDtypeStruct((b, half, s), F32)
    return pl.pallas_call(
        _rope_table_kernel,
        grid=(b,),
        in_specs=[pl.BlockSpec((1, 1, s), lambda i: (i, 0, 0)),
                  pl.BlockSpec((half, LANES), lambda i: (0, 0))],
        out_specs=[pl.BlockSpec((1, half, s), lambda i: (i, 0, 0))] * 2,
        out_shape=[out, out],
        compiler_params=_cparams(("arbitrary",)),
        name="rope_tables",
    )(positions.reshape(b, 1, s), invf)


_R_SBQ, _R_SBV, _R_DFQ, _R_DFK, _R_DFV, _R_FXQ, _R_FXV, _R_CQ, _R_CKV, _R_KR, _R_END = (
    0, 256, 512, 768, 1024, 1280, 1536, 1792, 2048, 2176, 2208)


def _rope_pair(x1, x2, cos, sin):
    return x1 * cos - x2 * sin, x2 * cos + x1 * sin


def _proj_kernel(h_ref, wt_ref, wn_ref, wf_ref, bf_ref, gq_ref, gkv_ref, wuq_ref, wukv_ref,
                 cos_ref, sin_ref,
                 sbq_ref, sbk_ref, sbv_ref, dfq_ref, dfk_ref, dfv_ref,
                 fxq_ref, fxk_ref, fxv_ref, fb_ref, mq_ref, mk_ref, mv_ref,
                 carry_ref):
    ts = h_ref.shape[1]
    nkb = ts // TK
    hb = h_ref[0].astype(BF16)
    cos = cos_ref[0]
    sin = sin_ref[0]

    def nt(lo, hi):
        return _dot_nt(wt_ref[lo:hi, :], hb)

    def store_vt(ref, x):
        for j in range(nkb):
            ref[0, j] = x[:, j * TK:(j + 1) * TK].astype(BF16)

    sbq_ref[0] = nt(_R_SBQ, _R_SBV).astype(BF16)
    store_vt(sbv_ref, nt(_R_SBV, _R_DFQ))
    fxq_ref[0] = nt(_R_FXQ, _R_FXV).astype(BF16)
    store_vt(fxv_ref, nt(_R_FXV, _R_CQ))
    kn = _dot(hb, wn_ref[...])
    sbk_ref[0] = kn[:, :256].astype(BF16)
    fxk_ref[0] = kn[:, 256:].astype(BF16)

    store_vt(dfv_ref, nt(_R_DFV, _R_FXQ))
    xq = nt(_R_DFQ, _R_DFK) * (1.0 / math.sqrt(DIFF_DIM))
    xk = nt(_R_DFK, _R_DFV)
    k_pieces = []
    for j in range(2 * DIFF_HEADS):
        lo = j * DIFF_DIM
        q1, q2 = _rope_pair(xq[lo:lo + 16], xq[lo + 16:lo + 32], cos, sin)
        dfq_ref[0, lo:lo + 16, :] = q1.astype(BF16)
        dfq_ref[0, lo + 16:lo + 32, :] = q2.astype(BF16)
        k1, k2 = _rope_pair(xk[lo:lo + 16], xk[lo + 16:lo + 32], cos, sin)
        k_pieces += [k1, k2]
    for c in range(2):
        kt = jnp.concatenate(k_pieces[8 * c:8 * c + 8], axis=0)
        dfk_ref[0, :, c * HEAD_PAIR:(c + 1) * HEAD_PAIR] = kt.T.astype(BF16)

    @pl.when(pl.program_id(1) == 0)
    def _():
        carry_ref[...] = jnp.zeros_like(carry_ref)

    f = _dot(hb, wf_ref[...]) + bf_ref[...]
    logf = jnp.minimum(f, 0.0) - jnp.log(1.0 + jnp.exp(-jnp.abs(f)))
    row = lax.broadcasted_iota(jnp.int32, (ts, ts), 0)
    col = lax.broadcasted_iota(jnp.int32, (ts, ts), 1)
    tri = (col <= row).astype(BF16)
    p0, p1, p2 = _split3(logf)
    cum = _dot(tri, p0) + _dot(tri, p1) + _dot(tri, p2) + carry_ref[0:1, :]
    for hh in range(FOX_HEADS):
        fb_ref[0, hh] = cum[:, hh * LANES:(hh + 1) * LANES]
    carry_ref[0:1, :] = cum[ts - 1:ts, :]

    cq = nt(_R_CQ, _R_CKV)
    cqn = cq * lax.rsqrt(jnp.mean(cq * cq, axis=0, keepdims=True) + RMS_EPS) * _tile_lanes(gq_ref[...], ts)
    qd = _dot(wuq_ref[...], cqn.astype(BF16)) * (1.0 / math.sqrt(MLA_NOPE_DIM + MLA_ROPE_DIM))
    for hh in range(MLA_HEADS):
        base = hh * HEAD_PAIR
        mq_ref[0, base:base + 64, :] = qd[base:base + 64].astype(BF16)
        r1, r2 = _rope_pair(qd[base + 64:base + 80], qd[base + 80:base + 96], cos, sin)
        mq_ref[0, base + 64:base + 80, :] = r1.astype(BF16)
        mq_ref[0, base + 80:base + 96, :] = r2.astype(BF16)
        mq_ref[0, base + 96:base + 128, :] = qd[base + 96:base + 128].astype(BF16)
    ckv = nt(_R_CKV, _R_KR)
    ckvn = ckv * lax.rsqrt(jnp.mean(ckv * ckv, axis=0, keepdims=True) + RMS_EPS) * _tile_lanes(gkv_ref[...], ts)
    kvd = _dot(wukv_ref[...], ckvn.astype(BF16))
    store_vt(mv_ref, kvd[256:512])
    kr = nt(_R_KR, _R_END)
    kr1, kr2 = _rope_pair(kr[0:16], kr[16:32], cos, sin)
    zpad = jnp.zeros((32, ts), F32)
    for hh in range(MLA_HEADS):
        kt = jnp.concatenate([kvd[hh * 64:(hh + 1) * 64], kr1, kr2, zpad], axis=0)
        mk_ref[0, :, hh * HEAD_PAIR:(hh + 1) * HEAD_PAIR] = kt.T.astype(BF16)


def _prep_proj_weights(w_in, b_forget, q_norm_g, w_uq, kv_norm_g, w_ukv):
    offs = [0]
    for c in IN_SPLITS:
        offs.append(offs[-1] + c)
    (sb_q, sb_k, sb_v, df_q, df_k, df_v, fx_q, fx_k, fx_v, fx_f, m_cq, m_ckv, m_kr) = [
        w_in[:, offs[i]:offs[i + 1]] for i in range(len(IN_SPLITS))]
    sb_scale = 1.0 / math.sqrt(SB_DIM)
    wt = jnp.concatenate([sb_q * sb_scale, sb_v, df_q, df_k, df_v, fx_q * sb_scale, fx_v,
                          m_cq, m_ckv, m_kr], axis=1).T.astype(BF16)
    wn = jnp.concatenate([sb_k, fx_k], axis=1).astype(BF16)
    wf = jnp.repeat(fx_f, LANES, axis=1).astype(BF16)
    bf = jnp.repeat(b_forget, LANES)[None, :].astype(F32)
    gq = jnp.broadcast_to(q_norm_g[:, None], (MLA_Q_RANK, LANES)).astype(F32)
    gkv = jnp.broadcast_to(kv_norm_g[:, None], (MLA_KV_RANK, LANES)).astype(F32)
    dq = MLA_NOPE_DIM + MLA_ROPE_DIM
    wuq = w_uq.reshape(MLA_Q_RANK, MLA_HEADS, dq)
    wuq = jnp.pad(wuq, ((0, 0), (0, 0), (0, HEAD_PAIR - dq)))
    wuq = wuq.reshape(MLA_Q_RANK, MLA_HEADS * HEAD_PAIR).T.astype(BF16)
    wukv = w_ukv.reshape(MLA_KV_RANK, MLA_HEADS, MLA_NOPE_DIM + MLA_V_DIM)
    wukv = jnp.concatenate([wukv[:, :, :MLA_NOPE_DIM].reshape(MLA_KV_RANK, -1),
                            wukv[:, :, MLA_NOPE_DIM:].reshape(MLA_KV_RANK, -1)], axis=1).T.astype(BF16)
    return wt, wn, wf, bf, gq, gkv, wuq, wukv


def _proj_call(h3, pw, cos_t, sin_t):
    b, s, d = h3.shape
    ts = TS_PROJ
    wt, wn, wf, bf, gq, gkv, wuq, wukv = pw
    nkb = s // TK

    def const(a):
        return pl.BlockSpec(a.shape, lambda i, j: (0,) * a.ndim)

    def fmaj(c):
        return (jax.ShapeDtypeStruct((b, c, s), BF16), pl.BlockSpec((1, c, ts), lambda i, j: (i, 0, j)))

    def tmaj(c):
        return (jax.ShapeDtypeStruct((b, s, c), BF16), pl.BlockSpec((1, ts, c), lambda i, j: (i, j, 0)))

    def vblk(c):
        return (jax.ShapeDtypeStruct((b, nkb, c, TK), BF16),
                pl.BlockSpec((1, ts // TK, c, TK), lambda i, j: (i, j, 0, 0)))

    fb = (jax.ShapeDtypeStruct((b, FOX_HEADS, s, LANES), F32),
          pl.BlockSpec((1, FOX_HEADS, ts, LANES), lambda i, j: (i, 0, j, 0)))
    outs = [fmaj(256), tmaj(256), vblk(256),
            fmaj(256), tmaj(256), vblk(256),
            fmaj(256), tmaj(256), vblk(256), fb,
            fmaj(512), tmaj(512), vblk(256)]
    return pl.pallas_call(
        _proj_kernel,
        grid=(b, s // ts),
        in_specs=[pl.BlockSpec((1, ts, d), lambda i, j: (i, j, 0)),
                  const(wt), const(wn), const(wf), const(bf), const(gq), const(gkv), const(wuq), const(wukv),
                  pl.BlockSpec((1, 16, ts), lambda i, j: (i, 0, j)),
                  pl.BlockSpec((1, 16, ts), lambda i, j: (i, 0, j))],
        out_specs=[o[1] for o in outs],
        out_shape=[o[0] for o in outs],
        scratch_shapes=[pltpu.VMEM((8, FOX_HEADS * LANES), F32)],
        compiler_params=_cparams(("arbitrary", "arbitrary")),
        name="in_proj",
    )(h3, wt, wn, wf, bf, gq, gkv, wuq, wukv, cos_t, sin_t)


def _sub_heads(mode):
    subs = []
    for hh in range(4):
        pair = HEAD_PAIR * (hh // 2)
        lo = 64 * (hh % 2)
        if mode == "mla":
            subs.append((HEAD_PAIR * hh, 0, HEAD_PAIR, HEAD_PAIR * hh, 64 * hh))
        elif mode == "diff":
            subs.append((pair, lo, lo + 32, pair, 64 * hh))
            subs.append((pair, lo + 32, lo + 64, pair, 64 * hh))
        else:
            subs.append((pair, lo, lo + 64, pair, 64 * hh))
    return tuple(subs)


def _q_variants(q, subs):
    out = []
    rows = lax.broadcasted_iota(jnp.int32, (HEAD_PAIR, q.shape[1]), 0)
    for (qlo, mlo, mhi, _, _) in subs:
        qa = q[qlo:qlo + HEAD_PAIR]
        if mhi - mlo < HEAD_PAIR:
            qa = jnp.where((rows >= mlo) & (rows < mhi), qa, jnp.zeros_like(qa))
        out.append(qa)
    return out


def _store_heads(o_ref, outs):
    for pr in range(2):
        pair = jnp.concatenate([outs[2 * pr], outs[2 * pr + 1]], axis=0)
        o_ref[0, :, pr * HEAD_PAIR:(pr + 1) * HEAD_PAIR] = pair.T.astype(o_ref.dtype)


def _key_query_iota():
    r = lax.broadcasted_iota(jnp.int32, (TK, TQ), 0)
    c = lax.broadcasted_iota(jnp.int32, (TK, TQ), 1)
    return r, c


def _softmax_attn_kernel(mode, subs, lambda_init, *refs):
    if mode == "fox":
        q_ref, k_ref, v_ref, fb_ref, o_ref, s_ref, cm_ref, m_ref, l_ref, acc_ref = refs
    elif mode == "diff":
        q_ref, k_ref, v_ref, dl_ref, g_ref, o_ref, s_ref, cm_ref, m_ref, l_ref, acc_ref = refs
    else:
        q_ref, k_ref, v_ref, o_ref, s_ref, cm_ref, m_ref, l_ref, acc_ref = refs
    qi = pl.program_id(1)
    nsub = len(subs)
    qs = _q_variants(q_ref[0], subs)
    m_ref[...] = jnp.full(m_ref.shape, NEG_BIG, F32)
    l_ref[...] = jnp.zeros_like(l_ref)
    acc_ref[...] = jnp.zeros_like(acc_ref)

    def stage_a(kj, diagonal):
        koff = pl.multiple_of(kj * TK, TK)
        kb = k_ref[0, pl.ds(koff, TK), :]
        for a, (_, _, _, klo, vlo) in enumerate(subs):
            st = _dot(kb[:, klo:klo + HEAD_PAIR], qs[a])
            if mode == "fox":
                st = st - _tile_lanes(fb_ref[0, vlo // 64, pl.ds(koff, TK), :], TQ)
            if diagonal:
                r, c = _key_query_iota()
                st = jnp.where(r <= c, st, NEG_BIG)
            s_ref[a] = st
            cm_ref[a] = jnp.max(st, axis=0, keepdims=True)

    def stage_b(kj):
        vb = v_ref[0, kj]
        for a, (_, _, _, _, vlo) in enumerate(subs):
            m_old = m_ref[a]
            m_new = jnp.maximum(m_old, cm_ref[a])
            alpha = jnp.exp(m_old - m_new)
            p = jnp.exp(s_ref[a] - m_new)
            l_ref[a] = alpha * l_ref[a] + jnp.sum(p, axis=0, keepdims=True)
            acc_ref[a] = alpha * acc_ref[a] + _dot(vb[vlo:vlo + 64], p.astype(BF16))
            m_ref[a] = m_new

    stage_a(qi, True)

    def body(t, carry):
        kj = qi - 1 - t
        stage_b(kj + 1)
        stage_a(kj, False)
        return carry

    lax.fori_loop(0, qi, body, 0)
    stage_b(0)

    outs = [acc_ref[a] / l_ref[a] for a in range(nsub)]
    if mode == "diff":
        dl = dl_ref[0]
        lam = (jnp.exp(jnp.sum(dl[0:1] * dl[1:2], axis=1, keepdims=True))
               - jnp.exp(jnp.sum(dl[2:3] * dl[3:4], axis=1, keepdims=True)) + lambda_init)
        g = _tile_lanes(g_ref[...], TQ)
        res = []
        for hh in range(4):
            d = outs[2 * hh] - lam * outs[2 * hh + 1]
            y = d * lax.rsqrt(jnp.mean(d * d, axis=0, keepdims=True) + RMS_EPS) * g
            res.append(y * (1.0 - lambda_init))
        outs = res
    _store_heads(o_ref, outs)


def _sb_attn_kernel(subs, q_ref, k_ref, v_ref, o_ref, s_ref, lk_ref, cs_ref, c_ref, acc_ref):
    qi = pl.program_id(1)
    qs = _q_variants(q_ref[0], subs)
    c_ref[...] = jnp.zeros_like(c_ref)
    acc_ref[...] = jnp.zeros_like(acc_ref)

    def stage_a(kj, diagonal):
        koff = pl.multiple_of(kj * TK, TK)
        kb = k_ref[0, pl.ds(koff, TK), :]
        for a, (_, _, _, klo, _) in enumerate(subs):
            z = _dot(kb[:, klo:klo + HEAD_PAIR], qs[a])
            lk = -(jnp.maximum(z, 0.0) + jnp.log(1.0 + jnp.exp(-jnp.abs(z))))
            zl = z + lk
            if diagonal:
                r, c = _key_query_iota()
                lk = jnp.where(r < c, lk, 0.0)
                zl = jnp.where(r < c, zl, NEG_BIG)
            s_ref[a] = zl
            lk_ref[a] = lk.astype(BF16)
            cs_ref[a] = jnp.sum(lk, axis=0, keepdims=True)

    def stage_b(kj):
        vb = v_ref[0, kj]
        r, c = _key_query_iota()
        later = (c > r).astype(BF16)
        for a, (_, _, _, _, vlo) in enumerate(subs):
            after = _dot(later, lk_ref[a])
            w = jnp.exp(s_ref[a] + after + c_ref[a])
            acc_ref[a] = acc_ref[a] + _dot(vb[vlo:vlo + 64], w.astype(BF16))
            c_ref[a] = c_ref[a] + cs_ref[a]

    stage_a(qi, True)

    def cond(state):
        t, live = state
        return (t < qi) & (live > 0)

    def body(state):
        t, _ = state
        kj = qi - 1 - t
        stage_b(kj + 1)
        stage_a(kj, False)
        live = (jnp.max(c_ref[...]) > -SB_UNDERFLOW).astype(jnp.int32)
        return t + 1, live

    t_end, _ = lax.while_loop(cond, body, (jnp.int32(0), jnp.int32(1)))
    stage_b(qi - t_end)
    _store_heads(o_ref, [acc_ref[a] for a in range(len(subs))])


def _attn_call(mode, lambda_init, q, k, v, extra=()):
    b, cq, s = q.shape
    ck = k.shape[2]
    subs = _sub_heads(mode)
    nsub = len(subs)
    in_specs = [pl.BlockSpec((1, cq, TQ), lambda i, j: (i, 0, j)),
                pl.BlockSpec((1, s, ck), lambda i, j: (i, 0, 0)),
                pl.BlockSpec((1, s // TK, 2 * HEAD_PAIR, TK), lambda i, j: (i, 0, 0, 0))]
    args = [q, k, v]
    tile = pltpu.VMEM((nsub, TK, TQ), F32)
    rowv = pltpu.VMEM((nsub, 1, TQ), F32)
    accv = pltpu.VMEM((nsub, 64, TQ), F32)
    if mode == "sb":
        kern = functools.partial(_sb_attn_kernel, subs)
        scratch = [tile, pltpu.VMEM((nsub, TK, TQ), BF16), rowv, rowv, accv]
    else:
        kern = functools.partial(_softmax_attn_kernel, mode, subs, lambda_init)
        scratch = [tile, rowv, rowv, rowv, accv]
        if mode == "fox":
            (fb,) = extra
            in_specs.append(pl.BlockSpec((1, FOX_HEADS, s, LANES), lambda i, j: (i, 0, 0, 0)))
            args.append(fb)
        elif mode == "diff":
            dl, g = extra
            in_specs += [pl.BlockSpec((1, 4, DIFF_DIM), lambda i, j: (0, 0, 0)),
                         pl.BlockSpec((DIFF_V_DIM, LANES), lambda i, j: (0, 0))]
            args += [dl, g]
    return pl.pallas_call(
        kern,
        grid=(b, s // TQ),
        in_specs=in_specs,
        out_specs=pl.BlockSpec((1, TQ, 2 * HEAD_PAIR), lambda i, j: (i, j, 0)),
        out_shape=jax.ShapeDtypeStruct((b, s, 2 * HEAD_PAIR), BF16),
        scratch_shapes=scratch,
        compiler_params=_cparams(("arbitrary", "arbitrary")),
        name="attn_" + mode,
    )(*args)


def _out_kernel(alpha, ma_ref, mb_ref, mc_ref, md_ref, w_ref, h_ref, g_ref, b_ref, o_ref):
    mix = _dot(ma_ref[...], w_ref[0:256, :])
    mix += _dot(mb_ref[...], w_ref[256:512, :])
    mix += _dot(mc_ref[...], w_ref[512:768, :])
    mix += _dot(md_ref[...], w_ref[768:1024, :])
    o_ref[...] = _layer_norm_rows(alpha * h_ref[...] + mix, g_ref[...], b_ref[...])


def _out_call(alpha, mixes, w_out_bf, h2d, g, b):
    n, d = h2d.shape
    tm = TM_OUT
    mspec = pl.BlockSpec((tm, 256), lambda i: (i, 0))
    return pl.pallas_call(
        functools.partial(_out_kernel, alpha),
        grid=(n // tm,),
        in_specs=[mspec, mspec, mspec, mspec,
                  pl.BlockSpec(w_out_bf.shape, lambda i: (0, 0)),
                  pl.BlockSpec((tm, d), lambda i: (i, 0)),
                  pl.BlockSpec((1, d), lambda i: (0, 0)),
                  pl.BlockSpec((1, d), lambda i: (0, 0))],
        out_specs=pl.BlockSpec((tm, d), lambda i: (i, 0)),
        out_shape=jax.ShapeDtypeStruct((n, d), F32),
        compiler_params=_cparams(("arbitrary",)),
        name="out_proj_ln",
    )(*mixes, w_out_bf, h2d, g.reshape(1, d), b.reshape(1, d))


def _router_kernel(h_ref, w_ref, b_ref, exp_ref, gate_ref, rank_ref, cnt_ref, carry_ref):
    tr = h_ref.shape[0]
    ne = w_ref.shape[0]

    @pl.when(pl.program_id(0) == 0)
    def _():
        carry_ref[...] = jnp.zeros_like(carry_ref)

    h0, h1, h2 = _split3(h_ref[...])
    w0, w1, w2 = _split3(w_ref[...])
    logits = (_dot_nt(w0, h0) + _dot_nt(w0, h1) + _dot_nt(w1, h0)
              + _dot_nt(w1, h1) + _dot_nt(w0, h2) + _dot_nt(w2, h0))
    logits = logits + _tile_lanes(b_ref[...], tr)
    eid = lax.broadcasted_iota(jnp.int32, (ne, tr), 0).astype(F32)
    vals, sels = [], []
    cur = logits
    for k in range(TOP_K):
        m = jnp.max(cur, axis=0, keepdims=True)
        idx = jnp.min(jnp.where(cur == m, eid, float(ne)), axis=0, keepdims=True)
        sel = eid == idx
        cur = jnp.where(sel, -jnp.inf, cur)
        vals.append(m)
        sels.append(sel)
        exp_ref[k:k + 1, :] = idx.astype(jnp.int32)
    es = [jnp.exp(v - vals[0]) for v in vals]
    den = es[0] + es[1] + es[2] + es[3]
    for k in range(TOP_K):
        gate_ref[k:k + 1, :] = es[k] / den
    exp_ref[TOP_K:, :] = jnp.zeros((8 - TOP_K, tr), jnp.int32)
    gate_ref[TOP_K:, :] = jnp.zeros((8 - TOP_K, tr), F32)
    rank_ref[TOP_K:, :] = jnp.zeros((8 - TOP_K, tr), jnp.int32)

    onehot = (sels[0] | sels[1] | sels[2] | sels[3]).astype(BF16)
    rj = lax.broadcasted_iota(jnp.int32, (tr, tr), 0)
    ct = lax.broadcasted_iota(jnp.int32, (tr, tr), 1)
    before = (rj < ct).astype(BF16)
    cum = _dot(onehot, before) + _tile_lanes(carry_ref[...], tr)
    for k in range(TOP_K):
        rk = jnp.sum(jnp.where(sels[k], cum, 0.0), axis=0, keepdims=True)
        rank_ref[k:k + 1, :] = rk.astype(jnp.int32)
    total = carry_ref[...] + _dot(onehot, jnp.ones((tr, LANES), BF16))
    carry_ref[...] = total
    cnt_ref[...] = total


def _router_call(h2d, router_w, router_b):
    n, d = h2d.shape
    ne = router_w.shape[1]
    tr = TR_ROUTER
    wt = router_w.T
    bb = jnp.broadcast_to(router_b[:, None], (ne, LANES))
    rows = jax.ShapeDtypeStruct((8, n), jnp.int32)
    return pl.pallas_call(
        _router_kernel,
        grid=(n // tr,),
        in_specs=[pl.BlockSpec((tr, d), lambda i: (i, 0)),
                  pl.BlockSpec((ne, d), lambda i: (0, 0)),
                  pl.BlockSpec((ne, LANES), lambda i: (0, 0))],
        out_specs=[pl.BlockSpec((8, tr), lambda i: (0, i)),
                   pl.BlockSpec((8, tr), lambda i: (0, i)),
                   pl.BlockSpec((8, tr), lambda i: (0, i)),
                   pl.BlockSpec((ne, LANES), lambda i: (0, 0))],
        out_shape=[rows, jax.ShapeDtypeStruct((8, n), F32), rows,
                   jax.ShapeDtypeStruct((ne, LANES), F32)],
        scratch_shapes=[pltpu.VMEM((ne, LANES), F32)],
        compiler_params=_cparams(("arbitrary",)),
        name="router",
    )(h2d, wt, bb)


def _row_copy(src_hbm, src_row, dst, dst_row, sem):
    return pltpu.make_async_copy(src_hbm.at[pl.ds(src_row, 1)], dst.at[pl.ds(dst_row, 1)], sem)


def _dispatch_kernel(slot_hbm, h_ref, xs_in_hbm, xs_hbm, idx_ref, sem_idx, sem):
    del xs_in_hbm
    td = TD_DISPATCH
    base = pl.program_id(0) * td
    cp = pltpu.make_async_copy(slot_hbm.at[pl.ds(base * TOP_K, td * TOP_K)], idx_ref, sem_idx)
    cp.start()
    cp.wait()

    def issue(j, carry):
        for k in range(TOP_K):
            _row_copy(h_ref, j, xs_hbm, idx_ref[j * TOP_K + k], sem).start()
        return carry

    lax.fori_loop(0, td, issue, 0)

    def drain(j, carry):
        for k in range(TOP_K):
            _row_copy(h_ref, j, xs_hbm, idx_ref[j * TOP_K + k], sem).wait()
        return carry

    lax.fori_loop(0, td, drain, 0)


def _dispatch_call(slot_flat, h2d, n_slots):
    n, d = h2d.shape
    td = TD_DISPATCH
    xs0 = jnp.zeros((n_slots, d), F32)
    anyspec = pl.BlockSpec(memory_space=pl.ANY)
    return pl.pallas_call(
        _dispatch_kernel,
        grid=(n // td,),
        in_specs=[anyspec, pl.BlockSpec((td, d), lambda i: (i, 0)), anyspec],
        out_specs=anyspec,
        out_shape=jax.ShapeDtypeStruct((n_slots, d), F32),
        scratch_shapes=[pltpu.SMEM((td * TOP_K,), jnp.int32),
                        pltpu.SemaphoreType.DMA(()), pltpu.SemaphoreType.DMA(())],
        input_output_aliases={2: 0},
        compiler_params=_cparams(("arbitrary",)),
        name="moe_dispatch",
    )(slot_flat, h2d, xs0)


def _expert_kernel(bexp_ref, nused_ref, x_ref, wgu_ref, bgu_ref, wd_ref, bd_ref, y_ref, wgu_s, wd_s):
    i = pl.program_id(0)
    dff = wd_ref.shape[2]

    @pl.when(i < nused_ref[0])
    def _():
        e = bexp_ref[i]
        prev = bexp_ref[jnp.maximum(i - 1, 0)]

        @pl.when((i == 0) | (e != prev))
        def _():
            wgu_s[...] = wgu_ref[0, 0].astype(BF16)
            wd_s[...] = wd_ref[0, 0].astype(BF16)

        xb = x_ref[...].astype(BF16)
        gu = _dot(xb, wgu_s[...]) + bgu_ref[0]
        glu = jnp.minimum(gu[:, :dff], SWIGLU_LIMIT)
        lin = jnp.clip(gu[:, dff:], -SWIGLU_LIMIT, SWIGLU_LIMIT)
        act = glu * (1.0 / (1.0 + jnp.exp(-SWIGLU_ALPHA * glu))) * (lin + 1.0)
        y_ref[...] = _dot(act.astype(BF16), wd_s[...]) + bd_ref[0]

    @pl.when(i >= nused_ref[0])
    def _():
        y_ref[...] = jnp.zeros_like(y_ref)


def _expert_call(layer, block_exp, n_used, xs, w_gate_up, b_gate_up, w_down, b_down):
    n_slots, d = xs.shape
    _, ne, _, dff2 = w_gate_up.shape
    dff = dff2 // 2
    nb = n_slots // MOE_BLOCK

    def blk(i, bexp, nused):
        return jnp.minimum(i, nused[0] - 1)

    def wmap(i, bexp, nused):
        return (layer, bexp[blk(i, bexp, nused)], 0, 0)

    def bmap(i, bexp, nused):
        return (layer * ne + bexp[blk(i, bexp, nused)], 0, 0)

    grid_spec = pltpu.PrefetchScalarGridSpec(
        num_scalar_prefetch=2,
        grid=(nb,),
        in_specs=[pl.BlockSpec((MOE_BLOCK, d), lambda i, bexp, nused: (blk(i, bexp, nused), 0)),
                  pl.BlockSpec((1, 1, d, dff2), wmap),
                  pl.BlockSpec((1, 1, dff2), bmap),
                  pl.BlockSpec((1, 1, dff, d), wmap),
                  pl.BlockSpec((1, 1, d), bmap)],
        out_specs=pl.BlockSpec((MOE_BLOCK, d), lambda i, bexp, nused: (i, 0)),
        scratch_shapes=[pltpu.VMEM((d, dff2), BF16), pltpu.VMEM((dff, d), BF16)],
    )
    return pl.pallas_call(
        _expert_kernel,
        grid_spec=grid_spec,
        out_shape=jax.ShapeDtypeStruct((n_slots, d), F32),
        compiler_params=_cparams(("arbitrary",)),
        name="moe_experts",
    )(block_exp, n_used, xs, w_gate_up, b_gate_up.reshape(-1, 1, dff2), w_down, b_down.reshape(-1, 1, d))


def _combine_kernel(alpha, slot_hbm, ys_hbm, gate_ref, h_ref, g_ref, b_ref, o_ref, idx_ref, buf_ref, sem_idx, sem):
    tc = TC_COMBINE
    base = pl.program_id(0) * tc
    cp = pltpu.make_async_copy(slot_hbm.at[pl.ds(base * TOP_K, tc * TOP_K)], idx_ref, sem_idx)
    cp.start()
    cp.wait()

    def issue(j, carry):
        for k in range(TOP_K):
            _row_copy(ys_hbm, idx_ref[j * TOP_K + k], buf_ref.at[k], j, sem).start()
        return carry

    lax.fori_loop(0, tc, issue, 0)

    def drain(j, carry):
        for k in range(TOP_K):
            _row_copy(ys_hbm, idx_ref[j * TOP_K + k], buf_ref.at[k], j, sem).wait()
        return carry

    lax.fori_loop(0, tc, drain, 0)

    gate = gate_ref[...]
    y = alpha * h_ref[...]
    for k in range(TOP_K):
        y = y + buf_ref[k] * gate[:, k:k + 1]
    o_ref[...] = _layer_norm_rows(y, g_ref[...], b_ref[...])


def _combine_call(alpha, slot_flat, ys, gate_tok, h2d, g, b):
    n, d = h2d.shape
    tc = TC_COMBINE
    anyspec = pl.BlockSpec(memory_space=pl.ANY)
    return pl.pallas_call(
        functools.partial(_combine_kernel, alpha),
        grid=(n // tc,),
        in_specs=[anyspec, anyspec,
                  pl.BlockSpec((tc, TOP_K), lambda i: (i, 0)),
                  pl.BlockSpec((tc, d), lambda i: (i, 0)),
                  pl.BlockSpec((1, d), lambda i: (0, 0)),
                  pl.BlockSpec((1, d), lambda i: (0, 0))],
        out_specs=pl.BlockSpec((tc, d), lambda i: (i, 0)),
        out_shape=jax.ShapeDtypeStruct((n, d), F32),
        scratch_shapes=[pltpu.SMEM((tc * TOP_K,), jnp.int32),
                        pltpu.VMEM((TOP_K, tc, d), F32),
                        pltpu.SemaphoreType.DMA(()), pltpu.SemaphoreType.DMA(())],
        compiler_params=_cparams(("arbitrary",)),
        name="moe_combine_ln",
    )(slot_flat, ys, gate_tok, h2d, g.reshape(1, d), b.reshape(1, d))


def _moe_layer(layer, alpha, h2d, router_w, router_b, w_gate_up, b_gate_up, w_down, b_down, g, b):
    n, d = h2d.shape
    ne = router_w.shape[1]
    exp_t, gate_t, rank_t, cnt = _router_call(h2d, router_w, router_b)
    counts = cnt[:, 0].astype(jnp.int32)
    padded = (counts + MOE_BLOCK - 1) // MOE_BLOCK * MOE_BLOCK
    pend = jnp.cumsum(padded)
    pstart = pend - padded
    onehot = exp_t[:TOP_K, :, None] == jnp.arange(ne, dtype=jnp.int32)[None, None, :]
    slot_t = rank_t[:TOP_K] + jnp.sum(jnp.where(onehot, pstart[None, None, :], 0), axis=-1)
    slot_flat = slot_t.T.reshape(-1)
    nb = n * TOP_K // MOE_BLOCK + ne
    n_used = (pend[-1] // MOE_BLOCK).astype(jnp.int32).reshape(1)
    bstart = jnp.arange(nb, dtype=jnp.int32) * MOE_BLOCK
    block_exp = jnp.minimum(jnp.sum(pend[None, :] <= bstart[:, None], axis=1), ne - 1).astype(jnp.int32)

    xs = _dispatch_call(slot_flat, h2d, nb * MOE_BLOCK)
    ys = _expert_call(layer, block_exp, n_used, xs, w_gate_up, b_gate_up, w_down, b_down)
    return _combine_call(alpha, slot_flat, ys, gate_t[:TOP_K].T, h2d, g, b)


def kernel(x, positions, ln_in_g, ln_in_b, w_in, b_forget, diff_lambda, diff_subln_g, mla_q_norm_g, mla_w_uq, mla_kv_norm_g, mla_w_ukv, w_out, ln1_g, ln1_b, router_w, router_b, w_gate_up, b_gate_up, w_down, b_down, ln2_g, ln2_b):
    bsz, s, d = x.shape
    depth = w_in.shape[0]
    n = bsz * s
    alpha = (2 * depth) ** 0.25
    cos_t, sin_t = _rope_tables(positions)
    h = _layer_norm_call(x.reshape(n, d), ln_in_g, ln_in_b)
    g_sub = jnp.broadcast_to(diff_subln_g[:, :, None], (depth, DIFF_V_DIM, LANES))
    for l in range(depth):
        lambda_init = 0.8 - 0.6 * math.exp(-0.3 * l)
        pw = _prep_proj_weights(w_in[l], b_forget[l], mla_q_norm_g[l], mla_w_uq[l], mla_kv_norm_g[l], mla_w_ukv[l])
        (sbq, sbk, sbv, dfq, dfk, dfv, fxq, fxk, fxv, fb, mq, mk, mv) = _proj_call(h.reshape(bsz, s, d), pw, cos_t, sin_t)
        mix_a = _attn_call("sb", lambda_init, sbq, sbk, sbv)
        mix_b = _attn_call("diff", lambda_init, dfq, dfk, dfv, (diff_lambda[l:l + 1], g_sub[l]))
        mix_c = _attn_call("fox", lambda_init, fxq, fxk, fxv, (fb,))
        mix_d = _attn_call("mla", lambda_init, mq, mk, mv)
        mixes = [m.reshape(n, 256) for m in (mix_a, mix_b, mix_c, mix_d)]
        h = _out_call(alpha, mixes, w_out[l].astype(BF16), h, ln1_g[l], ln1_b[l])
        h = _moe_layer(l, alpha, h, router_w[l], router_b[l], w_gate_up, b_gate_up, w_down, b_down, ln2_g[l], ln2_b[l])
    return h.reshape(bsz, s, d)
```

```python
import functools
import math

import jax
import jax.numpy as jnp
from jax import lax
from jax.experimental import pallas as pl
from jax.experimental.pallas import tpu as pltpu

F32 = jnp.float32
BF16 = jnp.bfloat16

SB_HEADS, SB_DIM = 4, 64
DIFF_HEADS, DIFF_DIM, DIFF_V_DIM = 4, 32, 64
FOX_HEADS, FOX_DIM = 4, 64
MLA_HEADS, MLA_Q_RANK, MLA_KV_RANK = 4, 256, 128
MLA_NOPE_DIM, MLA_ROPE_DIM, MLA_V_DIM = 64, 32, 64
TOP_K = 4
SWIGLU_LIMIT = 7.0
SWIGLU_ALPHA = 1.702
ROPE_THETA = 10000.0
LN_EPS = 1e-5
RMS_EPS = 1e-6
IN_SPLITS = (256, 256, 256, 256, 256, 256, 256, 256, 256, 4, 256, 128, 32)

LANES = 128
HEAD_PAIR = 128
VMEM_LIMIT = 56 * 1024 * 1024

TS_PROJ = 512
TQ = 256
TK = 256
TM_OUT = 512
TR_ROUTER = 512
MOE_BLOCK = 256
TD_DISPATCH = 256
TC_COMBINE = 256
ROWS_PER_TRIP = 8
NEG_BIG = -1e30
SB_UNDERFLOW = 104.0


def _cparams(sem):
    return pltpu.CompilerParams(dimension_semantics=sem, vmem_limit_bytes=VMEM_LIMIT)


def _dot(a, b):
    return jnp.dot(a, b, preferred_element_type=F32)


def _dot_nt(a, b):
    return lax.dot_general(a, b, (((1,), (1,)), ((), ())), preferred_element_type=F32)


def _split3(x):
    a = x.astype(BF16)
    r = x - a.astype(F32)
    b = r.astype(BF16)
    c = (r - b.astype(F32)).astype(BF16)
    return a, b, c


def _tile_lanes(x, n):
    return jnp.concatenate([x] * (n // LANES), axis=1) if n > LANES else x


def _layer_norm_rows(y, g, b):
    mu = jnp.mean(y, axis=-1, keepdims=True)
    d = y - mu
    var = jnp.mean(d * d, axis=-1, keepdims=True)
    return d * lax.rsqrt(var + LN_EPS) * g + b


def _ln_kernel(x_ref, g_ref, b_ref, o_ref):
    o_ref[...] = _layer_norm_rows(x_ref[...], g_ref[...], b_ref[...])


def _layer_norm_call(x2d, g, b):
    n, d = x2d.shape
    tm = 512
    return pl.pallas_call(
        _ln_kernel,
        grid=(n // tm,),
        in_specs=[pl.BlockSpec((tm, d), lambda i: (i, 0)),
                  pl.BlockSpec((1, d), lambda i: (0, 0)),
                  pl.BlockSpec((1, d), lambda i: (0, 0))],
        out_specs=pl.BlockSpec((tm, d), lambda i: (i, 0)),
        out_shape=jax.ShapeDtypeStruct((n, d), F32),
        compiler_params=_cparams(("arbitrary",)),
        name="ln_in",
    )(x2d, g.reshape(1, d), b.reshape(1, d))


def _rope_table_kernel(pos_ref, invf_ref, cos_ref, sin_ref):
    s = pos_ref.shape[-1]
    pos = pos_ref[0].astype(F32)
    ang = _tile_lanes(invf_ref[...], s) * pos
    cos_ref[0] = jnp.cos(ang)
    sin_ref[0] = jnp.sin(ang)


def _rope_tables(positions):
    b, s = positions.shape
    half = DIFF_DIM // 2
    inv_freq = ROPE_THETA ** (-jnp.arange(0, DIFF_DIM, 2, dtype=F32) / DIFF_DIM)
    invf = jnp.broadcast_to(inv_freq[:, None], (half, LANES))
    out = jax.ShapeDtypeStruct((b, half, s), F32)
    return pl.pallas_call(
        _rope_table_kernel,
        grid=(b,),
        in_specs=[pl.BlockSpec((1, 1, s), lambda i: (i, 0, 0)),
                  pl.BlockSpec((half, LANES), lambda i: (0, 0))],
        out_specs=[pl.BlockSpec((1, half, s), lambda i: (i, 0, 0))] * 2,
        out_shape=[out, out],
        compiler_params=_cparams(("arbitrary",)),
        name="rope_tables",
    )(positions.reshape(b, 1, s), invf)


(_R_SBQ, _R_SBV, _R_DFQ, _R_DFK, _R_DFV, _R_FXQ, _R_FXK, _R_FXV, _R_CQ, _R_CKV, _R_KR, _R_F, _R_END) = (
    0, 256, 512, 768, 1024, 1280, 1536, 1792, 2048, 2304, 2432, 2464, 2496)
LOG2E = 1.4426950408889634


def _rope_pair(x1, x2, cos, sin):
    return x1 * cos - x2 * sin, x2 * cos + x1 * sin


def _proj_kernel(h_ref, wt_ref, wn_ref, bf_ref, gq_ref, gkv_ref, wuq_ref, wukv_ref,
                 cos_ref, sin_ref,
                 sbq_ref, sbk_ref, sbv_ref, dfq_ref, dfk_ref, dfv_ref,
                 fxq_ref, fxk_ref, fxv_ref, mq_ref, mk_ref, mv_ref,
                 carry_ref):
    ts = h_ref.shape[1]
    nkb = ts // TK
    hb = h_ref[0].astype(BF16)
    cos = cos_ref[0]
    sin = sin_ref[0]

    def nt(lo, hi):
        return _dot_nt(wt_ref[lo:hi, :], hb)

    def store_vt(ref, x):
        for j in range(nkb):
            ref[0, j] = x[:, j * TK:(j + 1) * TK].astype(BF16)

    sbq_ref[0] = (nt(_R_SBQ, _R_SBV) * (1.0 / math.sqrt(SB_DIM))).astype(BF16)
    store_vt(sbv_ref, nt(_R_SBV, _R_DFQ))
    sbk_ref[0] = _dot(hb, wn_ref[...]).astype(BF16)

    @pl.when(pl.program_id(1) == 0)
    def _():
        carry_ref[...] = jnp.zeros_like(carry_ref)

    f = nt(_R_F, _R_END) + _tile_lanes(bf_ref[...], ts)
    logf = jnp.minimum(f, 0.0) - jnp.log(1.0 + jnp.exp(-jnp.abs(f)))
    row = lax.broadcasted_iota(jnp.int32, (ts, ts), 0)
    col = lax.broadcasted_iota(jnp.int32, (ts, ts), 1)
    tri = (row <= col).astype(BF16)
    ones = jnp.ones((ts, LANES), BF16)
    cum = _tile_lanes(carry_ref[...], ts)
    total = carry_ref[...]
    for piece in _split3(logf):
        cum = cum + _dot(piece, tri)
        total = total + _dot(piece, ones)
    carry_ref[...] = total
    fa, fb, fc = _split3(cum * (-LOG2E))
    r8 = lax.broadcasted_iota(jnp.int32, (8, ts), 0)
    one_rows = jnp.where(r8 < 3, 1.0, 0.0)
    xq = nt(_R_FXQ, _R_FXK) * (LOG2E / math.sqrt(FOX_DIM))
    xk = nt(_R_FXK, _R_FXV)
    zpad = jnp.zeros((HEAD_PAIR - FOX_DIM - 8, ts), F32)
    for hh in range(FOX_HEADS):
        lo, r0 = hh * FOX_DIM, hh * 8
        f_rows = jnp.where(r8 == 0, fa[r0:r0 + 8].astype(F32),
                           jnp.where(r8 == 1, fb[r0:r0 + 8].astype(F32),
                                     jnp.where(r8 == 2, fc[r0:r0 + 8].astype(F32), 0.0)))
        qt = jnp.concatenate([xq[lo:lo + FOX_DIM], one_rows, zpad], axis=0)
        fxq_ref[0, hh * HEAD_PAIR:(hh + 1) * HEAD_PAIR, :] = qt.astype(BF16)
        kt = jnp.concatenate([xk[lo:lo + FOX_DIM], f_rows, zpad], axis=0)
        fxk_ref[0, :, hh * HEAD_PAIR:(hh + 1) * HEAD_PAIR] = kt.T.astype(BF16)
    store_vt(fxv_ref, nt(_R_FXV, _R_CQ))

    store_vt(dfv_ref, nt(_R_DFV, _R_FXQ))
    xq = nt(_R_DFQ, _R_DFK) * (LOG2E / math.sqrt(DIFF_DIM))
    xk = nt(_R_DFK, _R_DFV)
    k_pieces = []
    for j in range(2 * DIFF_HEADS):
        lo = j * DIFF_DIM
        q1, q2 = _rope_pair(xq[lo:lo + 16], xq[lo + 16:lo + 32], cos, sin)
        dfq_ref[0, lo:lo + 16, :] = q1.astype(BF16)
        dfq_ref[0, lo + 16:lo + 32, :] = q2.astype(BF16)
        k1, k2 = _rope_pair(xk[lo:lo + 16], xk[lo + 16:lo + 32], cos, sin)
        k_pieces += [k1, k2]
    for c in range(2):
        kt = jnp.concatenate(k_pieces[8 * c:8 * c + 8], axis=0)
        dfk_ref[0, :, c * HEAD_PAIR:(c + 1) * HEAD_PAIR] = kt.T.astype(BF16)

    cq = nt(_R_CQ, _R_CKV)
    cqn = cq * lax.rsqrt(jnp.mean(cq * cq, axis=0, keepdims=True) + RMS_EPS) * _tile_lanes(gq_ref[...], ts)
    qd = _dot(wuq_ref[...], cqn.astype(BF16)) * (LOG2E / math.sqrt(MLA_NOPE_DIM + MLA_ROPE_DIM))
    for hh in range(MLA_HEADS):
        base = hh * HEAD_PAIR
        mq_ref[0, base:base + 64, :] = qd[base:base + 64].astype(BF16)
        r1, r2 = _rope_pair(qd[base + 64:base + 80], qd[base + 80:base + 96], cos, sin)
        mq_ref[0, base + 64:base + 80, :] = r1.astype(BF16)
        mq_ref[0, base + 80:base + 96, :] = r2.astype(BF16)
        mq_ref[0, base + 96:base + 128, :] = qd[base + 96:base + 128].astype(BF16)
    ckv = nt(_R_CKV, _R_KR)
    ckvn = ckv * lax.rsqrt(jnp.mean(ckv * ckv, axis=0, keepdims=True) + RMS_EPS) * _tile_lanes(gkv_ref[...], ts)
    kvd = _dot(wukv_ref[...], ckvn.astype(BF16))
    store_vt(mv_ref, kvd[256:512])
    kr = nt(_R_KR, _R_F)
    kr1, kr2 = _rope_pair(kr[0:16], kr[16:32], cos, sin)
    zpad = jnp.zeros((32, ts), F32)
    for hh in range(MLA_HEADS):
        kt = jnp.concatenate([kvd[hh * 64:(hh + 1) * 64], kr1, kr2, zpad], axis=0)
        mk_ref[0, :, hh * HEAD_PAIR:(hh + 1) * HEAD_PAIR] = kt.T.astype(BF16)


def _prep_proj_weights(w_in, b_forget, q_norm_g, w_uq, kv_norm_g, w_ukv):
    offs = [0]
    for c in IN_SPLITS:
        offs.append(offs[-1] + c)
    (sb_q, sb_k, sb_v, df_q, df_k, df_v, fx_q, fx_k, fx_v, fx_f, m_cq, m_ckv, m_kr) = [
        w_in[:, offs[i]:offs[i + 1]] for i in range(len(IN_SPLITS))]
    wt = jnp.concatenate([sb_q, sb_v, df_q, df_k, df_v, fx_q, fx_k, fx_v, m_cq, m_ckv, m_kr,
                          jnp.repeat(fx_f, 8, axis=1)], axis=1).T.astype(BF16)
    wn = sb_k.astype(BF16)
    bf = jnp.broadcast_to(jnp.repeat(b_forget, 8)[:, None], (8 * FOX_HEADS, LANES)).astype(F32)
    gq =jnp.broadcast_to(q_norm_g[:, None], (MLA_Q_RANK, LANES)).astype(F32)
    gkv = jnp.broadcast_to(kv_norm_g[:, None], (MLA_KV_RANK, LANES)).astype(F32)
    dq = MLA_NOPE_DIM + MLA_ROPE_DIM
    wuq = w_uq.reshape(MLA_Q_RANK, MLA_HEADS, dq)
    wuq = jnp.pad(wuq, ((0, 0), (0, 0), (0, HEAD_PAIR - dq)))
    wuq = wuq.reshape(MLA_Q_RANK, MLA_HEADS * HEAD_PAIR).T.astype(BF16)
    wukv = w_ukv.reshape(MLA_KV_RANK, MLA_HEADS, MLA_NOPE_DIM + MLA_V_DIM)
    wukv = jnp.concatenate([wukv[:, :, :MLA_NOPE_DIM].reshape(MLA_KV_RANK, -1),
                            wukv[:, :, MLA_NOPE_DIM:].reshape(MLA_KV_RANK, -1)], axis=1).T.astype(BF16)
    return wt, wn, bf, gq, gkv, wuq, wukv


def _proj_call(h3, pw, cos_t, sin_t):
    b, s, d = h3.shape
    ts = TS_PROJ
    wt, wn, bf, gq, gkv, wuq, wukv = pw
    nkb = s // TK

    def const(a):
        return pl.BlockSpec(a.shape, lambda i, j: (0,) * a.ndim)

    def fmaj(c):
        return (jax.ShapeDtypeStruct((b, c, s), BF16), pl.BlockSpec((1, c, ts), lambda i, j: (i, 0, j)))

    def tmaj(c):
        return (jax.ShapeDtypeStruct((b, s, c), BF16), pl.BlockSpec((1, ts, c), lambda i, j: (i, j, 0)))

    def vblk(c):
        return (jax.ShapeDtypeStruct((b, nkb, c, TK), BF16),
                pl.BlockSpec((1, ts // TK, c, TK), lambda i, j: (i, j, 0, 0)))

    outs = [fmaj(256), tmaj(256), vblk(256),
            fmaj(256), tmaj(256), vblk(256),
            fmaj(512), tmaj(512), vblk(256),
            fmaj(512), tmaj(512), vblk(256)]
    return pl.pallas_call(
        _proj_kernel,
        grid=(b, s // ts),
        in_specs=[pl.BlockSpec((1, ts, d), lambda i, j: (i, j, 0)),
                  const(wt), const(wn), const(bf), const(gq), const(gkv), const(wuq), const(wukv),
                  pl.BlockSpec((1, 16, ts), lambda i, j: (i, 0, j)),
                  pl.BlockSpec((1, 16, ts), lambda i, j: (i, 0, j))],
        out_specs=[o[1] for o in outs],
        out_shape=[o[0] for o in outs],
        scratch_shapes=[pltpu.VMEM((8 * FOX_HEADS, LANES), F32)],
        compiler_params=_cparams(("arbitrary", "arbitrary")),
        name="in_proj",
    )(h3, wt, wn, bf, gq, gkv, wuq, wukv, cos_t, sin_t)


def _sub_heads(mode):
    subs = []
    for hh in range(4):
        pair = HEAD_PAIR * (hh // 2)
        lo = 64 * (hh % 2)
        if mode == "wide":
            subs.append((HEAD_PAIR * hh, 0, HEAD_PAIR, HEAD_PAIR * hh, 64 * hh))
        elif mode == "diff":
            subs.append((pair, lo, lo + 32, pair, 64 * hh))
            subs.append((pair, lo + 32, lo + 64, pair, 64 * hh))
        else:
            subs.append((pair, lo, lo + 64, pair, 64 * hh))
    return tuple(subs)


def _q_variants(q, subs):
    out = []
    rows = lax.broadcasted_iota(jnp.int32, (HEAD_PAIR, q.shape[1]), 0)
    for (qlo, mlo, mhi, _, _) in subs:
        qa = q[qlo:qlo + HEAD_PAIR]
        if mhi - mlo < HEAD_PAIR:
            qa = jnp.where((rows >= mlo) & (rows < mhi), qa, jnp.zeros_like(qa))
        out.append(qa)
    return out


def _store_heads(o_ref, outs):
    for pr in range(2):
        pair = jnp.concatenate([outs[2 * pr], outs[2 * pr + 1]], axis=0)
        o_ref[0, :, pr * HEAD_PAIR:(pr + 1) * HEAD_PAIR] = pair.T.astype(o_ref.dtype)


def _key_query_iota():
    r = lax.broadcasted_iota(jnp.int32, (TK, TQ), 0)
    c = lax.broadcasted_iota(jnp.int32, (TK, TQ), 1)
    return r, c


def _run_key_blocks(qi, stage_a, stage_b, still_live):
    stage_a(qi, True, 0)

    if still_live is not None:
        def cond(state):
            t, live = state
            return (t < qi) & (live > 0)

        def step(state):
            t, _ = state
            stage_b(qi - t, 0)
            stage_a(qi - 1 - t, False, 0)
            return t + 1, still_live()

        t, _ = lax.while_loop(cond, step, (jnp.int32(0), jnp.int32(1)))
        stage_b(qi - t, 0)
        return

    def pair(i, carry):
        kj = qi - 1 - 2 * i
        stage_a(kj, False, 1)
        stage_b(kj + 1, 0)
        stage_a(kj - 1, False, 0)
        stage_b(kj, 1)
        return carry

    lax.fori_loop(0, qi // 2, pair, 0)
    odd_tail = qi % 2 == 1

    @pl.when(odd_tail)
    def _():
        stage_a(0, False, 1)
        stage_b(1, 0)
        stage_b(0, 1)

    @pl.when(jnp.logical_not(odd_tail))
    def _():
        stage_b(0, 0)


def _softmax_attn_kernel(mode, subs, lambda_init, *refs):
    if mode == "diff":
        q_ref, k_ref, v_ref, dl_ref, g_ref, o_ref, s_ref, cm_ref, m_ref, l_ref, acc_ref = refs
    else:
        q_ref, k_ref, v_ref, o_ref, s_ref, cm_ref, m_ref, l_ref, acc_ref = refs
    qi = pl.program_id(1)
    nsub = len(subs)
    qs = _q_variants(q_ref[0], subs)
    m_ref[...] = jnp.full(m_ref.shape, NEG_BIG, F32)
    l_ref[...] = jnp.zeros_like(l_ref)
    acc_ref[...] = jnp.zeros_like(acc_ref)

    def stage_a(kj, diagonal, slot):
        koff = pl.multiple_of(kj * TK, TK)
        kb = k_ref[0, pl.ds(koff, TK), :]
        for a, (_, _, _, klo, vlo) in enumerate(subs):
            st = _dot(kb[:, klo:klo + HEAD_PAIR], qs[a])
            if diagonal:
                r, c = _key_query_iota()
                st = jnp.where(r <= c, st, NEG_BIG)
            s_ref[slot, a] = st
            cm_ref[slot, a] = jnp.max(st, axis=0, keepdims=True)

    def stage_b(kj, slot):
        vb = v_ref[0, kj]
        for a, (_, _, _, _, vlo) in enumerate(subs):
            m_old = m_ref[a]
            m_new = jnp.maximum(m_old, cm_ref[slot, a])
            alpha = jnp.exp2(m_old - m_new)
            p = jnp.exp2(s_ref[slot, a] - m_new)
            l_ref[a] = alpha * l_ref[a] + jnp.sum(p, axis=0, keepdims=True)
            acc_ref[a] = alpha * acc_ref[a] + _dot(vb[vlo:vlo + 64], p.astype(BF16))
            m_ref[a] = m_new

    _run_key_blocks(qi, stage_a, stage_b, None)

    outs = [acc_ref[a] / l_ref[a] for a in range(nsub)]
    if mode == "diff":
        dl = dl_ref[0]
        lam = (jnp.exp(jnp.sum(dl[0:1] * dl[1:2], axis=1, keepdims=True))
               - jnp.exp(jnp.sum(dl[2:3] * dl[3:4], axis=1, keepdims=True)) + lambda_init)
        g = _tile_lanes(g_ref[...], TQ)
        res = []
        for hh in range(4):
            d = outs[2 * hh] - lam * outs[2 * hh + 1]
            y = d * lax.rsqrt(jnp.mean(d * d, axis=0, keepdims=True) + RMS_EPS) * g
            res.append(y * (1.0 - lambda_init))
        outs = res
    _store_heads(o_ref, outs)


def _sb_attn_kernel(subs, q_ref, k_ref, v_ref, o_ref, s_ref, lk_ref, cs_ref, c_ref, acc_ref):
    qi = pl.program_id(1)
    qs = _q_variants(q_ref[0], subs)
    c_ref[...] = jnp.zeros_like(c_ref)
    acc_ref[...] = jnp.zeros_like(acc_ref)

    def stage_a(kj, diagonal, slot):
        koff = pl.multiple_of(kj * TK, TK)
        kb = k_ref[0, pl.ds(koff, TK), :]
        for a, (_, _, _, klo, _) in enumerate(subs):
            z = _dot(kb[:, klo:klo + HEAD_PAIR], qs[a])
            lk = -(jnp.maximum(z, 0.0) + jnp.log(1.0 + jnp.exp(-jnp.abs(z))))
            zl = z + lk
            if diagonal:
                r, c = _key_query_iota()
                lk = jnp.where(r < c, lk, 0.0)
                zl = jnp.where(r < c, zl, NEG_BIG)
            s_ref[slot, a] = zl
            lk_ref[slot, a] = lk.astype(BF16)
            cs_ref[slot, a] = jnp.sum(lk, axis=0, keepdims=True)

    def stage_b(kj, slot):
        vb = v_ref[0, kj]
        r, c = _key_query_iota()
        later = (c > r).astype(BF16)
        for a, (_, _, _, _, vlo) in enumerate(subs):
            after = _dot(later, lk_ref[slot, a])
            w = jnp.exp(s_ref[slot, a] + after + c_ref[a])
            acc_ref[a] = acc_ref[a] + _dot(vb[vlo:vlo + 64], w.astype(BF16))
            c_ref[a] = c_ref[a] + cs_ref[slot, a]

    def still_live():
        return (jnp.max(c_ref[...]) > -SB_UNDERFLOW).astype(jnp.int32)

    _run_key_blocks(qi, stage_a, stage_b, still_live)
    _store_heads(o_ref, [acc_ref[a] for a in range(len(subs))])


def _attn_call(name, mode, lambda_init, q, k, v, extra=()):
    b, cq, s = q.shape
    ck = k.shape[2]
    subs = _sub_heads(mode)
    nsub = len(subs)
    in_specs = [pl.BlockSpec((1, cq, TQ), lambda i, j: (i, 0, j)),
                pl.BlockSpec((1, s, ck), lambda i, j: (i, 0, 0)),
                pl.BlockSpec((1, s // TK, 2 * HEAD_PAIR, TK), lambda i, j: (i, 0, 0, 0))]
    args = [q, k, v]
    tile = pltpu.VMEM((2, nsub, TK, TQ), F32)
    slot_rowv = pltpu.VMEM((2, nsub, 1, TQ), F32)
    rowv = pltpu.VMEM((nsub, 1, TQ), F32)
    accv = pltpu.VMEM((nsub, 64, TQ), F32)
    if mode == "sb":
        kern = functools.partial(_sb_attn_kernel, subs)
        scratch = [tile, pltpu.VMEM((2, nsub, TK, TQ), BF16), slot_rowv, rowv, accv]
    else:
        kern = functools.partial(_softmax_attn_kernel, mode, subs, lambda_init)
        scratch = [tile, slot_rowv, rowv, rowv, accv]
        if mode == "diff":
            dl, g = extra
            in_specs += [pl.BlockSpec((1, 4, DIFF_DIM), lambda i, j: (0, 0, 0)),
                         pl.BlockSpec((DIFF_V_DIM, LANES), lambda i, j: (0, 0))]
            args += [dl, g]
    return pl.pallas_call(
        kern,
        grid=(b, s // TQ),
        in_specs=in_specs,
        out_specs=pl.BlockSpec((1, TQ, 2 * HEAD_PAIR), lambda i, j: (i, j, 0)),
        out_shape=jax.ShapeDtypeStruct((b, s, 2 * HEAD_PAIR), BF16),
        scratch_shapes=scratch,
        compiler_params=_cparams(("arbitrary", "arbitrary")),
        name="attn_" + name,
    )(*args)


def _out_kernel(alpha, ma_ref, mb_ref, mc_ref, md_ref, w_ref, h_ref, g_ref, b_ref, o_ref):
    mix = _dot(ma_ref[...], w_ref[0:256, :])
    mix += _dot(mb_ref[...], w_ref[256:512, :])
    mix += _dot(mc_ref[...], w_ref[512:768, :])
    mix += _dot(md_ref[...], w_ref[768:1024, :])
    o_ref[...] = _layer_norm_rows(alpha * h_ref[...] + mix, g_ref[...], b_ref[...])


def _out_call(alpha, mixes, w_out_bf, h2d, g, b):
    n, d = h2d.shape
    tm = TM_OUT
    mspec = pl.BlockSpec((tm, 256), lambda i: (i, 0))
    return pl.pallas_call(
        functools.partial(_out_kernel, alpha),
        grid=(n // tm,),
        in_specs=[mspec, mspec, mspec, mspec,
                  pl.BlockSpec(w_out_bf.shape, lambda i: (0, 0)),
                  pl.BlockSpec((tm, d), lambda i: (i, 0)),
                  pl.BlockSpec((1, d), lambda i: (0, 0)),
                  pl.BlockSpec((1, d), lambda i: (0, 0))],
        out_specs=pl.BlockSpec((tm, d), lambda i: (i, 0)),
        out_shape=jax.ShapeDtypeStruct((n, d), F32),
        compiler_params=_cparams(("arbitrary",)),
        name="out_proj_ln",
    )(*mixes, w_out_bf, h2d, g.reshape(1, d), b.reshape(1, d))


def _router_kernel(h_ref, w_ref, b_ref, exp_ref, gate_ref, rank_ref, cnt_ref, carry_ref):
    tr = h_ref.shape[0]
    ne = w_ref.shape[0]

    @pl.when(pl.program_id(0) == 0)
    def _():
        carry_ref[...] = jnp.zeros_like(carry_ref)

    h0, h1, h2 = _split3(h_ref[...])
    w0, w1, w2 = _split3(w_ref[...])
    logits = (_dot_nt(w0, h0) + _dot_nt(w0, h1) + _dot_nt(w1, h0)
              + _dot_nt(w1, h1) + _dot_nt(w0, h2) + _dot_nt(w2, h0))
    logits = logits + _tile_lanes(b_ref[...], tr)
    eid = lax.broadcasted_iota(jnp.int32, (ne, tr), 0).astype(F32)
    vals, sels = [], []
    cur = logits
    for k in range(TOP_K):
        m = jnp.max(cur, axis=0, keepdims=True)
        idx = jnp.min(jnp.where(cur == m, eid, float(ne)), axis=0, keepdims=True)
        sel = eid == idx
        cur = jnp.where(sel, -jnp.inf, cur)
        vals.append(m)
        sels.append(sel)
        exp_ref[k:k + 1, :] = idx.astype(jnp.int32)
    es = [jnp.exp(v - vals[0]) for v in vals]
    den = es[0] + es[1] + es[2] + es[3]
    for k in range(TOP_K):
        gate_ref[k:k + 1, :] = es[k] / den
    exp_ref[TOP_K:, :] = jnp.zeros((8 - TOP_K, tr), jnp.int32)
    gate_ref[TOP_K:, :] = jnp.zeros((8 - TOP_K, tr), F32)
    rank_ref[TOP_K:, :] = jnp.zeros((8 - TOP_K, tr), jnp.int32)

    onehot = (sels[0] | sels[1] | sels[2] | sels[3]).astype(BF16)
    rj = lax.broadcasted_iota(jnp.int32, (tr, tr), 0)
    ct = lax.broadcasted_iota(jnp.int32, (tr, tr), 1)
    before = (rj < ct).astype(BF16)
    cum = _dot(onehot, before) + _tile_lanes(carry_ref[...], tr)
    for k in range(TOP_K):
        rk = jnp.sum(jnp.where(sels[k], cum, 0.0), axis=0, keepdims=True)
        rank_ref[k:k + 1, :] = rk.astype(jnp.int32)
    total = carry_ref[...] + _dot(onehot, jnp.ones((tr, LANES), BF16))
    carry_ref[...] = total
    cnt_ref[...] = total


def _router_call(h2d, router_w, router_b):
    n, d = h2d.shape
    ne = router_w.shape[1]
    tr = TR_ROUTER
    wt = router_w.T
    bb = jnp.broadcast_to(router_b[:, None], (ne, LANES))
    rows = jax.ShapeDtypeStruct((8, n), jnp.int32)
    return pl.pallas_call(
        _router_kernel,
        grid=(n // tr,),
        in_specs=[pl.BlockSpec((tr, d), lambda i: (i, 0)),
                  pl.BlockSpec((ne, d), lambda i: (0, 0)),
                  pl.BlockSpec((ne, LANES), lambda i: (0, 0))],
        out_specs=[pl.BlockSpec((8, tr), lambda i: (0, i)),
                   pl.BlockSpec((8, tr), lambda i: (0, i)),
                   pl.BlockSpec((8, tr), lambda i: (0, i)),
                   pl.BlockSpec((ne, LANES), lambda i: (0, 0))],
        out_shape=[rows, jax.ShapeDtypeStruct((8, n), F32), rows,
                   jax.ShapeDtypeStruct((ne, LANES), F32)],
        scratch_shapes=[pltpu.VMEM((ne, LANES), F32)],
        compiler_params=_cparams(("arbitrary",)),
        name="router",
    )(h2d, wt, bb)


def _row_copy(src_hbm, src_row, dst, dst_row, sem):
    return pltpu.make_async_copy(src_hbm.at[pl.ds(src_row, 1)], dst.at[pl.ds(dst_row, 1)], sem)


def _for_each_row_copy(n_tokens, make_copy, act):
    def trip(jj, carry):
        j0 = pl.multiple_of(jj * ROWS_PER_TRIP, ROWS_PER_TRIP)
        for s in range(ROWS_PER_TRIP):
            for k in range(TOP_K):
                act(make_copy(j0 + s, k), k)
        return carry

    lax.fori_loop(0, n_tokens // ROWS_PER_TRIP, trip, 0)


def _dispatch_kernel(zblock_ref, slot_hbm, h_ref, xs_hbm, idx_ref, zero_ref, sem_idx, sem, sem_zero):
    td = TD_DISPATCH
    base = pl.program_id(0) * td

    @pl.when(pl.program_id(0) == 0)
    def _():
        zero_ref[...] = jnp.zeros_like(zero_ref)

        def zero_copy(e):
            z0 = pl.multiple_of(zblock_ref[e] * MOE_BLOCK, MOE_BLOCK)
            return pltpu.make_async_copy(zero_ref, xs_hbm.at[pl.ds(z0, MOE_BLOCK)], sem_zero)

        for act in ("start", "wait"):
            for e in range(zblock_ref.shape[0]):
                @pl.when(zblock_ref[e] >= 0)
                def _():
                    getattr(zero_copy(e), act)()

    cp = pltpu.make_async_copy(slot_hbm.at[pl.ds(base * TOP_K, td * TOP_K)], idx_ref, sem_idx)
    cp.start()
    cp.wait()

    def make_copy(j, k):
        return _row_copy(h_ref, j, xs_hbm, idx_ref[j * TOP_K + k], sem)

    _for_each_row_copy(td, make_copy, lambda cp_, k: cp_.start(priority=k % 2))
    _for_each_row_copy(td, make_copy, lambda cp_, k: cp_.wait())


def _dispatch_call(zblock, slot_flat, h2d, n_slots):
    n, d = h2d.shape
    td = TD_DISPATCH
    anyspec = pl.BlockSpec(memory_space=pl.ANY)
    grid_spec = pltpu.PrefetchScalarGridSpec(
        num_scalar_prefetch=1,
        grid=(n // td,),
        in_specs=[anyspec, pl.BlockSpec((td, d), lambda i, z: (i, 0))],
        out_specs=anyspec,
        scratch_shapes=[pltpu.SMEM((td * TOP_K,), jnp.int32),
                        pltpu.VMEM((MOE_BLOCK, d), F32),
                        pltpu.SemaphoreType.DMA(()), pltpu.SemaphoreType.DMA(()), pltpu.SemaphoreType.DMA(())],
    )
    return pl.pallas_call(
        _dispatch_kernel,
        grid_spec=grid_spec,
        out_shape=jax.ShapeDtypeStruct((n_slots, d), F32),
        compiler_params=_cparams(("arbitrary",)),
        name="moe_dispatch",
    )(zblock, slot_flat, h2d)


def _expert_kernel(layer, bexp_ref, nexp_ref, nused_ref, x_ref, wgu_hbm, bgu_ref, wd_hbm, bd_ref, y_ref,
                   wgu_f, wd_f, wgu_s, wd_s, sem_gu, sem_d):
    i = pl.program_id(0)
    dff = wd_s.shape[0]

    def weight_copies(e):
        return (pltpu.make_async_copy(wgu_hbm.at[layer, e], wgu_f, sem_gu),
                pltpu.make_async_copy(wd_hbm.at[layer, e], wd_f, sem_d))

    @pl.when(i < nused_ref[0])
    def _():
        e = bexp_ref[i]
        prev = bexp_ref[jnp.maximum(i - 1, 0)]

        @pl.when(i == 0)
        def _():
            for cp in weight_copies(e):
                cp.start()

        @pl.when((i == 0) | (e != prev))
        def _():
            for cp in weight_copies(e):
                cp.wait()
            wgu_s[...] = wgu_f[...].astype(BF16)
            wd_s[...] = wd_f[...].astype(BF16)
            nxt = nexp_ref[i]

            @pl.when(nxt >= 0)
            def _():
                for cp in weight_copies(nxt):
                    cp.start()

        xb = x_ref[...].astype(BF16)
        gu = _dot(xb, wgu_s[...]) + bgu_ref[0]
        glu = jnp.minimum(gu[:, :dff], SWIGLU_LIMIT)
        lin = jnp.clip(gu[:, dff:], -SWIGLU_LIMIT, SWIGLU_LIMIT)
        act = glu * (1.0 / (1.0 + jnp.exp(-SWIGLU_ALPHA * glu))) * (lin + 1.0)
        y_ref[...] = _dot(act.astype(BF16), wd_s[...]) + bd_ref[0]

    @pl.when(i >= nused_ref[0])
    def _():
        y_ref[...] = jnp.zeros_like(y_ref)


def _expert_call(layer, block_exp, next_exp, n_used, xs, w_gate_up, b_gate_up, w_down, b_down):
    n_slots, d = xs.shape
    _, ne, _, dff2 = w_gate_up.shape
    dff = dff2 // 2
    nb = n_slots // MOE_BLOCK

    def blk(i, bexp, nexp, nused):
        return jnp.minimum(i, nused[0] - 1)

    def bmap(i, bexp, nexp, nused):
        return (layer * ne + bexp[blk(i, bexp, nexp, nused)], 0, 0)

    anyspec = pl.BlockSpec(memory_space=pl.ANY)
    grid_spec = pltpu.PrefetchScalarGridSpec(
        num_scalar_prefetch=3,
        grid=(nb,),
        in_specs=[pl.BlockSpec((MOE_BLOCK, d), lambda i, bexp, nexp, nused: (blk(i, bexp, nexp, nused), 0)),
                  anyspec,
                  pl.BlockSpec((1, 1, dff2), bmap),
                  anyspec,
                  pl.BlockSpec((1, 1, d), bmap)],
        out_specs=pl.BlockSpec((MOE_BLOCK, d), lambda i, bexp, nexp, nused: (i, 0)),
        scratch_shapes=[pltpu.VMEM((d, dff2), F32), pltpu.VMEM((dff, d), F32),
                        pltpu.VMEM((d, dff2), BF16), pltpu.VMEM((dff, d), BF16),
                        pltpu.SemaphoreType.DMA(()), pltpu.SemaphoreType.DMA(())],
    )
    return pl.pallas_call(
        functools.partial(_expert_kernel, layer),
        grid_spec=grid_spec,
        out_shape=jax.ShapeDtypeStruct((n_slots, d), F32),
        compiler_params=_cparams(("arbitrary",)),
        name="moe_experts",
    )(block_exp, next_exp, n_used, xs, w_gate_up, b_gate_up.reshape(-1, 1, dff2), w_down,
      b_down.reshape(-1, 1, d))


def _combine_kernel(alpha, slot_hbm, ys_hbm, gate_ref, h_ref, g_ref, b_ref, o_ref, idx_ref, buf_ref, sem_idx, sem):
    tc = TC_COMBINE
    base = pl.program_id(0) * tc
    cp = pltpu.make_async_copy(slot_hbm.at[pl.ds(base * TOP_K, tc * TOP_K)], idx_ref, sem_idx)
    cp.start()
    cp.wait()

    def make_copy(j, k):
        return _row_copy(ys_hbm, idx_ref[j * TOP_K + k], buf_ref.at[k], j, sem)

    _for_each_row_copy(tc, make_copy, lambda cp_, k: cp_.start(priority=k % 2))
    _for_each_row_copy(tc, make_copy, lambda cp_, k: cp_.wait())

    gate = gate_ref[...]
    y = alpha * h_ref[...]
    for k in range(TOP_K):
        y = y + buf_ref[k] * gate[:, k:k + 1]
    o_ref[...] = _layer_norm_rows(y, g_ref[...], b_ref[...])


def _combine_call(alpha, slot_flat, ys, gate_tok, h2d, g, b):
    n, d = h2d.shape
    tc = TC_COMBINE
    anyspec = pl.BlockSpec(memory_space=pl.ANY)
    return pl.pallas_call(
        functools.partial(_combine_kernel, alpha),
        grid=(n // tc,),
        in_specs=[anyspec, anyspec,
                  pl.BlockSpec((tc, TOP_K), lambda i: (i, 0)),
                  pl.BlockSpec((tc, d), lambda i: (i, 0)),
                  pl.BlockSpec((1, d), lambda i: (0, 0)),
                  pl.BlockSpec((1, d), lambda i: (0, 0))],
        out_specs=pl.BlockSpec((tc, d), lambda i: (i, 0)),
        out_shape=jax.ShapeDtypeStruct((n, d), F32),
        scratch_shapes=[pltpu.SMEM((tc * TOP_K,), jnp.int32),
                        pltpu.VMEM((TOP_K, tc, d), F32),
                        pltpu.SemaphoreType.DMA(()), pltpu.SemaphoreType.DMA(())],
        compiler_params=_cparams(("arbitrary",)),
        name="moe_combine_ln",
    )(slot_flat, ys, gate_tok, h2d, g.reshape(1, d), b.reshape(1, d))


def _moe_layer(layer, alpha, h2d, router_w, router_b, w_gate_up, b_gate_up, w_down, b_down, g, b):
    n, d = h2d.shape
    ne = router_w.shape[1]
    exp_t, gate_t, rank_t, cnt = _router_call(h2d, router_w, router_b)
    counts = cnt[:, 0].astype(jnp.int32)
    padded = (counts + MOE_BLOCK - 1) // MOE_BLOCK * MOE_BLOCK
    pend = jnp.cumsum(padded)
    pstart = pend - padded
    onehot = exp_t[:TOP_K, :, None] == jnp.arange(ne, dtype=jnp.int32)[None, None, :]
    slot_t = rank_t[:TOP_K] + jnp.sum(jnp.where(onehot, pstart[None, None, :], 0), axis=-1)
    slot_flat = slot_t.T.reshape(-1)
    nb = n * TOP_K // MOE_BLOCK + ne
    n_used = (pend[-1] // MOE_BLOCK).astype(jnp.int32).reshape(1)
    bstart = jnp.arange(nb, dtype=jnp.int32) * MOE_BLOCK
    block_exp = jnp.minimum(jnp.sum(pend[None, :] <= bstart[:, None], axis=1), ne - 1).astype(jnp.int32)

    n_slots = nb * MOE_BLOCK
    last_blk = jnp.where(padded > 0, pend // MOE_BLOCK - 1, -1)
    tail = n_used[0] + jnp.arange(nb - n * TOP_K // MOE_BLOCK, dtype=jnp.int32)
    zblock = jnp.concatenate([last_blk, jnp.where(tail < nb, tail, -1)]).astype(jnp.int32)
    xs = _dispatch_call(zblock, slot_flat, h2d, n_slots)
    eid = jnp.arange(ne, dtype=jnp.int32)
    later_used = (padded[None, :] > 0) & (eid[None, :] > eid[:, None])
    nxt = jnp.min(jnp.where(later_used, eid[None, :], ne), axis=1)
    nxt = jnp.where(nxt == ne, -1, nxt)
    next_exp = jnp.sum(jnp.where(block_exp[:, None] == eid[None, :], nxt[None, :], 0), axis=1).astype(jnp.int32)
    ys = _expert_call(layer, block_exp, next_exp, n_used, xs, w_gate_up, b_gate_up, w_down, b_down)
    return _combine_call(alpha, slot_flat, ys, gate_t[:TOP_K].T, h2d, g, b)


def kernel(x, positions, ln_in_g, ln_in_b, w_in, b_forget, diff_lambda, diff_subln_g, mla_q_norm_g, mla_w_uq, mla_kv_norm_g, mla_w_ukv, w_out, ln1_g, ln1_b, router_w, router_b, w_gate_up, b_gate_up, w_down, b_down, ln2_g, ln2_b):
    bsz, s, d = x.shape
    depth = w_in.shape[0]
    n = bsz * s
    alpha = (2 * depth) ** 0.25
    cos_t, sin_t = _rope_tables(positions)
    h = _layer_norm_call(x.reshape(n, d), ln_in_g, ln_in_b)
    g_sub = jnp.broadcast_to(diff_subln_g[:, :, None], (depth, DIFF_V_DIM, LANES))
    for l in range(depth):
        lambda_init = 0.8 - 0.6 * math.exp(-0.3 * l)
        pw = _prep_proj_weights(w_in[l], b_forget[l], mla_q_norm_g[l], mla_w_uq[l], mla_kv_norm_g[l], mla_w_ukv[l])
        (sbq, sbk, sbv, dfq, dfk, dfv, fxq, fxk, fxv, mq, mk, mv) = _proj_call(h.reshape(bsz, s, d), pw, cos_t, sin_t)
        mix_a = _attn_call("sb", "sb", lambda_init, sbq, sbk, sbv)
        mix_b = _attn_call("diff", "diff", lambda_init, dfq, dfk, dfv, (diff_lambda[l:l + 1], g_sub[l]))
        mix_c = _attn_call("fox", "wide", lambda_init, fxq, fxk, fxv)
        mix_d = _attn_call("mla", "wide", lambda_init, mq, mk, mv)
        mixes = [m.reshape(n, 256) for m in (mix_a, mix_b, mix_c, mix_d)]
        h = _out_call(alpha, mixes, w_out[l].astype(BF16), h, ln1_g[l], ln1_b[l])
        h = _moe_layer(l, alpha, h, router_w[l], router_b[l], w_gate_up, b_gate_up, w_down, b_down, ln2_g[l], ln2_b[l])
    return h.reshape(bsz, s, d)
```

```python
import functools
import math

import jax
import jax.numpy as jnp
from jax import lax
from jax.experimental import pallas as pl
from jax.experimental.pallas import tpu as pltpu

F32 = jnp.float32
BF16 = jnp.bfloat16

SB_HEADS, SB_DIM = 4, 64
DIFF_HEADS, DIFF_DIM, DIFF_V_DIM = 4, 32, 64
FOX_HEADS, FOX_DIM = 4, 64
MLA_HEADS, MLA_Q_RANK, MLA_KV_RANK = 4, 256, 128
MLA_NOPE_DIM, MLA_ROPE_DIM, MLA_V_DIM = 64, 32, 64
TOP_K = 4
SWIGLU_LIMIT = 7.0
SWIGLU_ALPHA = 1.702
ROPE_THETA = 10000.0
LN_EPS = 1e-5
RMS_EPS = 1e-6
IN_SPLITS = (256, 256, 256, 256, 256, 256, 256, 256, 256, 4, 256, 128, 32)

LANES = 128
HEAD_PAIR = 128
VMEM_LIMIT = 56 * 1024 * 1024

TS_PROJ = 512
TQ = 256
TK = 256
TM_OUT = 1024
MOE_BLOCK = 512
TD_DISPATCH = 256
TC_COMBINE = 256
ROWS_PER_TRIP = 8
DMA_CHUNK = 128
NEG_BIG = -1e30
SB_UNDERFLOW = 104.0


def _cparams(sem):
    return pltpu.CompilerParams(dimension_semantics=sem, vmem_limit_bytes=VMEM_LIMIT)


def _dot(a, b):
    return jnp.dot(a, b, preferred_element_type=F32)


def _dot_nt(a, b):
    return lax.dot_general(a, b, (((1,), (1,)), ((), ())), preferred_element_type=F32)


def _split3(x):
    a = x.astype(BF16)
    r = x - a.astype(F32)
    b = r.astype(BF16)
    c = (r - b.astype(F32)).astype(BF16)
    return a, b, c


def _tile_lanes(x, n):
    return jnp.concatenate([x] * (n // LANES), axis=1) if n > LANES else x


def _layer_norm_rows(y, g, b):
    mu = jnp.mean(y, axis=-1, keepdims=True)
    d = y - mu
    var = jnp.mean(d * d, axis=-1, keepdims=True)
    return d * lax.rsqrt(var + LN_EPS) * g + b


def _ln_kernel(x_ref, g_ref, b_ref, o_ref):
    o_ref[...] = _layer_norm_rows(x_ref[...], g_ref[...], b_ref[...])


def _layer_norm_call(x2d, g, b):
    n, d = x2d.shape
    tm = 512
    return pl.pallas_call(
        _ln_kernel,
        grid=(n // tm,),
        in_specs=[pl.BlockSpec((tm, d), lambda i: (i, 0)),
                  pl.BlockSpec((1, d), lambda i: (0, 0)),
                  pl.BlockSpec((1, d), lambda i: (0, 0))],
        out_specs=pl.BlockSpec((tm, d), lambda i: (i, 0)),
        out_shape=jax.ShapeDtypeStruct((n, d), F32),
        compiler_params=_cparams(("arbitrary",)),
        name="ln_in",
    )(x2d, g.reshape(1, d), b.reshape(1, d))


def _rope_table_kernel(pos_ref, invf_ref, cos_ref, sin_ref):
    s = pos_ref.shape[-1]
    pos = pos_ref[0].astype(F32)
    ang = _tile_lanes(invf_ref[...], s) * pos
    cos_ref[0] = jnp.cos(ang)
    sin_ref[0] = jnp.sin(ang)


def _rope_tables(positions):
    b, s = positions.shape
    half = DIFF_DIM // 2
    inv_freq = ROPE_THETA ** (-jnp.arange(0, DIFF_DIM, 2, dtype=F32) / DIFF_DIM)
    invf = jnp.broadcast_to(inv_freq[:, None], (half, LANES))
    out = jax.ShapeDtypeStruct((b, half, s), F32)
    return pl.pallas_call(
        _rope_table_kernel,
        grid=(b,),
        in_specs=[pl.BlockSpec((1, 1, s), lambda i: (i, 0, 0)),
                  pl.BlockSpec((half, LANES), lambda i: (0, 0))],
        out_specs=[pl.BlockSpec((1, half, s), lambda i: (i, 0, 0))] * 2,
        out_shape=[out, out],
        compiler_params=_cparams(("arbitrary",)),
        name="rope_tables",
    )(positions.reshape(b, 1, s), invf)


(_R_SBQ, _R_SBV, _R_DFQ, _R_DFK, _R_DFV, _R_FXQ, _R_FXK, _R_FXV, _R_CQ, _R_CKV, _R_KR, _R_F, _R_END) = (
    0, 256, 512, 768, 1024, 1280, 1536, 1792, 2048, 2304, 2432, 2464, 2496)
LOG2E = 1.4426950408889634


def _rope_pair(x1, x2, cos, sin):
    return x1 * cos - x2 * sin, x2 * cos + x1 * sin


def _proj_kernel(h_ref, wt_ref, wn_ref, bf_ref, gq_ref, gkv_ref, wuq_ref, wukv_ref,
                 cos_ref, sin_ref,
                 sbq_ref, sbk_ref, sbv_ref, dfq_ref, dfk_ref, dfv_ref,
                 fxq_ref, fxk_ref, fxv_ref, mq_ref, mk_ref, mv_ref,
                 carry_ref):
    ts = h_ref.shape[1]
    nkb = ts // TK
    hb = h_ref[0].astype(BF16)
    cos = cos_ref[0]
    sin = sin_ref[0]

    def nt(lo, hi):
        return _dot_nt(wt_ref[lo:hi, :], hb)

    def store_vt(ref, x):
        for j in range(nkb):
            ref[0, j] = x[:, j * TK:(j + 1) * TK].astype(BF16)

    sbq_ref[0] = (nt(_R_SBQ, _R_SBV) * (1.0 / math.sqrt(SB_DIM))).astype(BF16)
    store_vt(sbv_ref, nt(_R_SBV, _R_DFQ))
    sbk_ref[0] = _dot(hb, wn_ref[...]).astype(BF16)

    @pl.when(pl.program_id(1) == 0)
    def _():
        carry_ref[...] = jnp.zeros_like(carry_ref)

    f = nt(_R_F, _R_END) + _tile_lanes(bf_ref[...], ts)
    logf = jnp.minimum(f, 0.0) - jnp.log(1.0 + jnp.exp(-jnp.abs(f)))
    row = lax.broadcasted_iota(jnp.int32, (ts, ts), 0)
    col = lax.broadcasted_iota(jnp.int32, (ts, ts), 1)
    tri = (row <= col).astype(BF16)
    ones = jnp.ones((ts, LANES), BF16)
    cum = _tile_lanes(carry_ref[...], ts)
    total = carry_ref[...]
    for piece in _split3(logf):
        cum = cum + _dot(piece, tri)
        total = total + _dot(piece, ones)
    carry_ref[...] = total
    fa, fb, fc = _split3(cum * (-LOG2E))
    r8 = lax.broadcasted_iota(jnp.int32, (8, ts), 0)
    one_rows = jnp.where(r8 < 3, 1.0, 0.0)
    xq = nt(_R_FXQ, _R_FXK) * (LOG2E / math.sqrt(FOX_DIM))
    xk = nt(_R_FXK, _R_FXV)
    zpad = jnp.zeros((HEAD_PAIR - FOX_DIM - 8, ts), F32)
    for hh in range(FOX_HEADS):
        lo, r0 = hh * FOX_DIM, hh * 8
        f_rows = jnp.where(r8 == 0, fa[r0:r0 + 8].astype(F32),
                           jnp.where(r8 == 1, fb[r0:r0 + 8].astype(F32),
                                     jnp.where(r8 == 2, fc[r0:r0 + 8].astype(F32), 0.0)))
        qt = jnp.concatenate([xq[lo:lo + FOX_DIM], one_rows, zpad], axis=0)
        fxq_ref[0, hh * HEAD_PAIR:(hh + 1) * HEAD_PAIR, :] = qt.astype(BF16)
        kt = jnp.concatenate([xk[lo:lo + FOX_DIM], f_rows, zpad], axis=0)
        fxk_ref[0, :, hh * HEAD_PAIR:(hh + 1) * HEAD_PAIR] = kt.T.astype(BF16)
    store_vt(fxv_ref, nt(_R_FXV, _R_CQ))

    store_vt(dfv_ref, nt(_R_DFV, _R_FXQ))
    xq = nt(_R_DFQ, _R_DFK) * (LOG2E / math.sqrt(DIFF_DIM))
    xk = nt(_R_DFK, _R_DFV)
    k_pieces = []
    for j in range(2 * DIFF_HEADS):
        lo = j * DIFF_DIM
        q1, q2 = _rope_pair(xq[lo:lo + 16], xq[lo + 16:lo + 32], cos, sin)
        dfq_ref[0, lo:lo + 16, :] = q1.astype(BF16)
        dfq_ref[0, lo + 16:lo + 32, :] = q2.astype(BF16)
        k1, k2 = _rope_pair(xk[lo:lo + 16], xk[lo + 16:lo + 32], cos, sin)
        k_pieces += [k1, k2]
    for c in range(2):
        kt = jnp.concatenate(k_pieces[8 * c:8 * c + 8], axis=0)
        dfk_ref[0, :, c * HEAD_PAIR:(c + 1) * HEAD_PAIR] = kt.T.astype(BF16)

    cq = nt(_R_CQ, _R_CKV)
    cqn = cq * lax.rsqrt(jnp.mean(cq * cq, axis=0, keepdims=True) + RMS_EPS) * _tile_lanes(gq_ref[...], ts)
    qd = _dot(wuq_ref[...], cqn.astype(BF16)) * (LOG2E / math.sqrt(MLA_NOPE_DIM + MLA_ROPE_DIM))
    for hh in range(MLA_HEADS):
        base = hh * HEAD_PAIR
        mq_ref[0, base:base + 64, :] = qd[base:base + 64].astype(BF16)
        r1, r2 = _rope_pair(qd[base + 64:base + 80], qd[base + 80:base + 96], cos, sin)
        mq_ref[0, base + 64:base + 80, :] = r1.astype(BF16)
        mq_ref[0, base + 80:base + 96, :] = r2.astype(BF16)
        mq_ref[0, base + 96:base + 128, :] = qd[base + 96:base + 128].astype(BF16)
    ckv = nt(_R_CKV, _R_KR)
    ckvn = ckv * lax.rsqrt(jnp.mean(ckv * ckv, axis=0, keepdims=True) + RMS_EPS) * _tile_lanes(gkv_ref[...], ts)
    kvd = _dot(wukv_ref[...], ckvn.astype(BF16))
    store_vt(mv_ref, kvd[256:512])
    kr = nt(_R_KR, _R_F)
    kr1, kr2 = _rope_pair(kr[0:16], kr[16:32], cos, sin)
    zpad = jnp.zeros((32, ts), F32)
    for hh in range(MLA_HEADS):
        kt = jnp.concatenate([kvd[hh * 64:(hh + 1) * 64], kr1, kr2, zpad], axis=0)
        mk_ref[0, :, hh * HEAD_PAIR:(hh + 1) * HEAD_PAIR] = kt.T.astype(BF16)


def _prep_proj_weights(w_in, b_forget, q_norm_g, w_uq, kv_norm_g, w_ukv):
    offs = [0]
    for c in IN_SPLITS:
        offs.append(offs[-1] + c)
    (sb_q, sb_k, sb_v, df_q, df_k, df_v, fx_q, fx_k, fx_v, fx_f, m_cq, m_ckv, m_kr) = [
        w_in[:, offs[i]:offs[i + 1]] for i in range(len(IN_SPLITS))]
    wt = jnp.concatenate([sb_q, sb_v, df_q, df_k, df_v, fx_q, fx_k, fx_v, m_cq, m_ckv, m_kr,
                          jnp.repeat(fx_f, 8, axis=1)], axis=1).T.astype(BF16)
    wn = sb_k.astype(BF16)
    bf = jnp.broadcast_to(jnp.repeat(b_forget, 8)[:, None], (8 * FOX_HEADS, LANES)).astype(F32)
    gq =jnp.broadcast_to(q_norm_g[:, None], (MLA_Q_RANK, LANES)).astype(F32)
    gkv = jnp.broadcast_to(kv_norm_g[:, None], (MLA_KV_RANK, LANES)).astype(F32)
    dq = MLA_NOPE_DIM + MLA_ROPE_DIM
    wuq = w_uq.reshape(MLA_Q_RANK, MLA_HEADS, dq)
    wuq = jnp.pad(wuq, ((0, 0), (0, 0), (0, HEAD_PAIR - dq)))
    wuq = wuq.reshape(MLA_Q_RANK, MLA_HEADS * HEAD_PAIR).T.astype(BF16)
    wukv = w_ukv.reshape(MLA_KV_RANK, MLA_HEADS, MLA_NOPE_DIM + MLA_V_DIM)
    wukv = jnp.concatenate([wukv[:, :, :MLA_NOPE_DIM].reshape(MLA_KV_RANK, -1),
                            wukv[:, :, MLA_NOPE_DIM:].reshape(MLA_KV_RANK, -1)], axis=1).T.astype(BF16)
    return wt, wn, bf, gq, gkv, wuq, wukv


def _proj_call(h3, pw, cos_t, sin_t):
    b, s, d = h3.shape
    ts = TS_PROJ
    wt, wn, bf, gq, gkv, wuq, wukv = pw
    nkb = s // TK

    def const(a):
        return pl.BlockSpec(a.shape, lambda i, j: (0,) * a.ndim)

    def fmaj(c):
        return (jax.ShapeDtypeStruct((b, c, s), BF16), pl.BlockSpec((1, c, ts), lambda i, j: (i, 0, j)))

    def tmaj(c):
        return (jax.ShapeDtypeStruct((b, s, c), BF16), pl.BlockSpec((1, ts, c), lambda i, j: (i, j, 0)))

    def vblk(c):
        return (jax.ShapeDtypeStruct((b, nkb, c, TK), BF16),
                pl.BlockSpec((1, ts // TK, c, TK), lambda i, j: (i, j, 0, 0)))

    outs = [fmaj(256), tmaj(256), vblk(256),
            fmaj(256), tmaj(256), vblk(256),
            fmaj(512), tmaj(512), vblk(256),
            fmaj(512), tmaj(512), vblk(256)]
    return pl.pallas_call(
        _proj_kernel,
        grid=(b, s // ts),
        in_specs=[pl.BlockSpec((1, ts, d), lambda i, j: (i, j, 0)),
                  const(wt), const(wn), const(bf), const(gq), const(gkv), const(wuq), const(wukv),
                  pl.BlockSpec((1, 16, ts), lambda i, j: (i, 0, j)),
                  pl.BlockSpec((1, 16, ts), lambda i, j: (i, 0, j))],
        out_specs=[o[1] for o in outs],
        out_shape=[o[0] for o in outs],
        scratch_shapes=[pltpu.VMEM((8 * FOX_HEADS, LANES), F32)],
        compiler_params=_cparams(("arbitrary", "arbitrary")),
        name="in_proj",
    )(h3, wt, wn, bf, gq, gkv, wuq, wukv, cos_t, sin_t)


def _sub_heads(mode):
    subs = []
    for hh in range(4):
        pair = HEAD_PAIR * (hh // 2)
        lo = 64 * (hh % 2)
        if mode == "wide":
            subs.append((HEAD_PAIR * hh, 0, HEAD_PAIR, HEAD_PAIR * hh, 64 * hh))
        elif mode == "diff":
            subs.append((pair, lo, lo + 32, pair, 64 * hh))
            subs.append((pair, lo + 32, lo + 64, pair, 64 * hh))
        else:
            subs.append((pair, lo, lo + 64, pair, 64 * hh))
    return tuple(subs)


def _q_variants(q, subs):
    out = []
    rows = lax.broadcasted_iota(jnp.int32, (HEAD_PAIR, q.shape[1]), 0)
    for (qlo, mlo, mhi, _, _) in subs:
        qa = q[qlo:qlo + HEAD_PAIR]
        if mhi - mlo < HEAD_PAIR:
            qa = jnp.where((rows >= mlo) & (rows < mhi), qa, jnp.zeros_like(qa))
        out.append(qa)
    return out


def _store_heads(o_ref, outs):
    for pr in range(2):
        pair = jnp.concatenate([outs[2 * pr], outs[2 * pr + 1]], axis=0)
        o_ref[0, :, pr * HEAD_PAIR:(pr + 1) * HEAD_PAIR] = pair.T.astype(o_ref.dtype)


def _key_query_iota():
    r = lax.broadcasted_iota(jnp.int32, (TK, TQ), 0)
    c = lax.broadcasted_iota(jnp.int32, (TK, TQ), 1)
    return r, c


def _run_key_blocks(qi, stage_a, stage_b, still_live):
    stage_a(qi, True, 0)

    if still_live is not None:
        @pl.when(qi == 0)
        def _():
            stage_b(0, 0)

        @pl.when(qi > 0)
        def _():
            stage_a(qi - 1, False, 1)
            stage_b(qi, 0)

            def cond(state):
                return state[1] > 0

            def step(state):
                kj, _ = state
                stage_b(kj, 1)
                more = (kj > 0) & (still_live() > 0)

                @pl.when(more)
                def _():
                    stage_a(kj - 1, False, 1)

                return kj - 1, more.astype(jnp.int32)

            lax.while_loop(cond, step, (qi - 1, jnp.int32(1)))
        return

    def pair(i, carry):
        kj = qi - 1 - 2 * i
        stage_a(kj, False, 1)
        stage_b(kj + 1, 0)
        stage_a(kj - 1, False, 0)
        stage_b(kj, 1)
        return carry

    lax.fori_loop(0, qi // 2, pair, 0)
    odd_tail = qi % 2 == 1

    @pl.when(odd_tail)
    def _():
        stage_a(0, False, 1)
        stage_b(1, 0)
        stage_b(0, 1)

    @pl.when(jnp.logical_not(odd_tail))
    def _():
        stage_b(0, 0)


def _softmax_attn_kernel(mode, subs, lambda_init, *refs):
    if mode == "diff":
        q_ref, k_ref, v_ref, dl_ref, g_ref, o_ref, s_ref, cm_ref, m_ref, l_ref, acc_ref = refs
    else:
        q_ref, k_ref, v_ref, o_ref, s_ref, cm_ref, m_ref, l_ref, acc_ref = refs
    qi = pl.program_id(1)
    nsub = len(subs)
    qs = _q_variants(q_ref[0], subs)
    m_ref[...] = jnp.full(m_ref.shape, NEG_BIG, F32)
    l_ref[...] = jnp.zeros_like(l_ref)
    acc_ref[...] = jnp.zeros_like(acc_ref)

    def stage_a(kj, diagonal, slot):
        koff = pl.multiple_of(kj * TK, TK)
        kb = k_ref[0, pl.ds(koff, TK), :]
        for a, (_, _, _, klo, vlo) in enumerate(subs):
            st = _dot(kb[:, klo:klo + HEAD_PAIR], qs[a])
            if diagonal:
                r, c = _key_query_iota()
                st = jnp.where(r <= c, st, NEG_BIG)
            s_ref[slot, a] = st
            cm_ref[slot, a] = jnp.max(st, axis=0, keepdims=True)

    def stage_b(kj, slot):
        vb = v_ref[0, kj]
        for a, (_, _, _, _, vlo) in enumerate(subs):
            m_old = m_ref[a]
            m_new = jnp.maximum(m_old, cm_ref[slot, a])
            alpha = jnp.exp2(m_old - m_new)
            p = jnp.exp2(s_ref[slot, a] - m_new)
            l_ref[a] = alpha * l_ref[a] + jnp.sum(p, axis=0, keepdims=True)
            acc_ref[a] = alpha * acc_ref[a] + _dot(vb[vlo:vlo + 64], p.astype(BF16))
            m_ref[a] = m_new

    _run_key_blocks(qi, stage_a, stage_b, None)

    outs = [acc_ref[a] / l_ref[a] for a in range(nsub)]
    if mode == "diff":
        dl = dl_ref[0]
        lam = (jnp.exp(jnp.sum(dl[0:1] * dl[1:2], axis=1, keepdims=True))
               - jnp.exp(jnp.sum(dl[2:3] * dl[3:4], axis=1, keepdims=True)) + lambda_init)
        g = _tile_lanes(g_ref[...], TQ)
        res = []
        for hh in range(4):
            d = outs[2 * hh] - lam * outs[2 * hh + 1]
            y = d * lax.rsqrt(jnp.mean(d * d, axis=0, keepdims=True) + RMS_EPS) * g
            res.append(y * (1.0 - lambda_init))
        outs = res
    _store_heads(o_ref, outs)


def _sb_attn_kernel(subs, q_ref, k_ref, v_ref, o_ref, s_ref, lk_ref, cs_ref, c_ref, acc_ref):
    qi = pl.program_id(1)
    qs = _q_variants(q_ref[0], subs)
    c_ref[...] = jnp.zeros_like(c_ref)
    acc_ref[...] = jnp.zeros_like(acc_ref)

    def stage_a(kj, diagonal, slot):
        koff = pl.multiple_of(kj * TK, TK)
        kb = k_ref[0, pl.ds(koff, TK), :]
        for a, (_, _, _, klo, _) in enumerate(subs):
            z = _dot(kb[:, klo:klo + HEAD_PAIR], qs[a])
            lk = -(jnp.maximum(z, 0.0) + jnp.log(1.0 + jnp.exp(-jnp.abs(z))))
            zl = z + lk
            if diagonal:
                r, c = _key_query_iota()
                lk = jnp.where(r < c, lk, 0.0)
                zl = jnp.where(r < c, zl, NEG_BIG)
            s_ref[slot, a] = zl
            lk_ref[slot, a] = lk.astype(BF16)
            cs_ref[slot, a] = jnp.sum(lk, axis=0, keepdims=True)

    def stage_b(kj, slot):
        vb = v_ref[0, kj]
        r, c = _key_query_iota()
        later = (c > r).astype(BF16)
        for a, (_, _, _, _, vlo) in enumerate(subs):
            after = _dot(later, lk_ref[slot, a])
            w = jnp.exp(s_ref[slot, a] + after + c_ref[a])
            acc_ref[a] = acc_ref[a] + _dot(vb[vlo:vlo + 64], w.astype(BF16))
            c_ref[a] = c_ref[a] + cs_ref[slot, a]

    def still_live():
        return (jnp.max(c_ref[...]) > -SB_UNDERFLOW).astype(jnp.int32)

    _run_key_blocks(qi, stage_a, stage_b, still_live)
    _store_heads(o_ref, [acc_ref[a] for a in range(len(subs))])


def _attn_call(name, mode, lambda_init, q, k, v, extra=()):
    b, cq, s = q.shape
    ck = k.shape[2]
    subs = _sub_heads(mode)
    nsub = len(subs)
    in_specs = [pl.BlockSpec((1, cq, TQ), lambda i, j: (i, 0, j)),
                pl.BlockSpec((1, s, ck), lambda i, j: (i, 0, 0)),
                pl.BlockSpec((1, s // TK, 2 * HEAD_PAIR, TK), lambda i, j: (i, 0, 0, 0))]
    args = [q, k, v]
    tile = pltpu.VMEM((2, nsub, TK, TQ), F32)
    slot_rowv = pltpu.VMEM((2, nsub, 1, TQ), F32)
    rowv = pltpu.VMEM((nsub, 1, TQ), F32)
    accv = pltpu.VMEM((nsub, 64, TQ), F32)
    if mode == "sb":
        kern = functools.partial(_sb_attn_kernel, subs)
        scratch = [tile, pltpu.VMEM((2, nsub, TK, TQ), BF16), slot_rowv, rowv, accv]
    else:
        kern = functools.partial(_softmax_attn_kernel, mode, subs, lambda_init)
        scratch = [tile, slot_rowv, rowv, rowv, accv]
        if mode == "diff":
            dl, g = extra
            in_specs += [pl.BlockSpec((1, 4, DIFF_DIM), lambda i, j: (0, 0, 0)),
                         pl.BlockSpec((DIFF_V_DIM, LANES), lambda i, j: (0, 0))]
            args += [dl, g]
    return pl.pallas_call(
        kern,
        grid=(b, s // TQ),
        in_specs=in_specs,
        out_specs=pl.BlockSpec((1, TQ, 2 * HEAD_PAIR), lambda i, j: (i, j, 0)),
        out_shape=jax.ShapeDtypeStruct((b, s, 2 * HEAD_PAIR), BF16),
        scratch_shapes=scratch,
        compiler_params=_cparams(("arbitrary", "arbitrary")),
        name="attn_" + name,
    )(*args)


def _out_kernel(alpha, ma_ref, mb_ref, mc_ref, md_ref, w_ref, h_ref, g_ref, b_ref, rw_ref, rb_ref,
                o_ref, exp_ref, gate_ref, rank_ref, cnt_ref, carry_ref):
    mix = _dot(ma_ref[...], w_ref[0:256, :])
    mix += _dot(mb_ref[...], w_ref[256:512, :])
    mix += _dot(mc_ref[...], w_ref[512:768, :])
    mix += _dot(md_ref[...], w_ref[768:1024, :])
    h1 = _layer_norm_rows(alpha * h_ref[...] + mix, g_ref[...], b_ref[...])
    o_ref[...] = h1
    _route(h1, rw_ref, rb_ref, exp_ref, gate_ref, rank_ref, cnt_ref, carry_ref)


def _out_call(alpha, mixes, w_out_bf, h2d, g, b, router_w, router_b):
    n, d = h2d.shape
    ne = router_w.shape[1]
    tm = TM_OUT
    mspec = pl.BlockSpec((tm, 256), lambda i: (i, 0))
    rowspec = pl.BlockSpec((8, tm), lambda i: (0, i))
    rows = jax.ShapeDtypeStruct((8, n), jnp.int32)
    wt = router_w.T
    bb = jnp.broadcast_to(router_b[:, None], (ne, LANES))
    return pl.pallas_call(
        functools.partial(_out_kernel, alpha),
        grid=(n // tm,),
        in_specs=[mspec, mspec, mspec, mspec,
                  pl.BlockSpec(w_out_bf.shape, lambda i: (0, 0)),
                  pl.BlockSpec((tm, d), lambda i: (i, 0)),
                  pl.BlockSpec((1, d), lambda i: (0, 0)),
                  pl.BlockSpec((1, d), lambda i: (0, 0)),
                  pl.BlockSpec((ne, d), lambda i: (0, 0)),
                  pl.BlockSpec((ne, LANES), lambda i: (0, 0))],
        out_specs=[pl.BlockSpec((tm, d), lambda i: (i, 0)), rowspec, rowspec, rowspec,
                   pl.BlockSpec((ne, LANES), lambda i: (0, 0))],
        out_shape=[jax.ShapeDtypeStruct((n, d), F32), rows, jax.ShapeDtypeStruct((8, n), F32), rows,
                   jax.ShapeDtypeStruct((ne, LANES), F32)],
        scratch_shapes=[pltpu.VMEM((ne, LANES), F32)],
        compiler_params=_cparams(("arbitrary",)),
        name="out_proj_ln_route",
    )(*mixes, w_out_bf, h2d, g.reshape(1, d), b.reshape(1, d), wt, bb)


def _route(h, w_ref, b_ref, exp_ref, gate_ref, rank_ref, cnt_ref, carry_ref):
    tr = h.shape[0]
    ne = w_ref.shape[0]

    @pl.when(pl.program_id(0) == 0)
    def _():
        carry_ref[...] = jnp.zeros_like(carry_ref)

    h0, h1, h2 = _split3(h)
    w0, w1, w2 = _split3(w_ref[...])
    wa = jnp.concatenate([w0, w1, w2], axis=0)
    r0 = _dot_nt(wa, h0)
    r1 = _dot_nt(wa[0:2 * ne], h1)
    logits = (r0[0:ne] + r0[ne:2 * ne] + r0[2 * ne:3 * ne] + r1[0:ne] + r1[ne:2 * ne]
              + _dot_nt(w0, h2))
    logits = logits + _tile_lanes(b_ref[...], tr)
    eid = lax.broadcasted_iota(jnp.int32, (ne, tr), 0).astype(F32)
    vals, sels = [], []
    cur = logits
    for k in range(TOP_K):
        m = jnp.max(cur, axis=0, keepdims=True)
        idx = jnp.min(jnp.where(cur == m, eid, float(ne)), axis=0, keepdims=True)
        sel = eid == idx
        cur = jnp.where(sel, -jnp.inf, cur)
        vals.append(m)
        sels.append(sel)
        exp_ref[k:k + 1, :] = idx.astype(jnp.int32)
    es = [jnp.exp(v - vals[0]) for v in vals]
    den = es[0] + es[1] + es[2] + es[3]
    for k in range(TOP_K):
        gate_ref[k:k + 1, :] = es[k] / den
    exp_ref[TOP_K:, :] = jnp.zeros((8 - TOP_K, tr), jnp.int32)
    gate_ref[TOP_K:, :] = jnp.zeros((8 - TOP_K, tr), F32)
    rank_ref[TOP_K:, :] = jnp.zeros((8 - TOP_K, tr), jnp.int32)

    onehot = (sels[0] | sels[1] | sels[2] | sels[3]).astype(BF16)
    rj = lax.broadcasted_iota(jnp.int32, (tr, tr), 0)
    ct = lax.broadcasted_iota(jnp.int32, (tr, tr), 1)
    before = (rj < ct).astype(BF16)
    cum = _dot(onehot, before) + _tile_lanes(carry_ref[...], tr)
    for k in range(TOP_K):
        rk = jnp.sum(jnp.where(sels[k], cum, 0.0), axis=0, keepdims=True)
        rank_ref[k:k + 1, :] = rk.astype(jnp.int32)
    total = carry_ref[...] + _dot(onehot, jnp.ones((tr, LANES), BF16))
    carry_ref[...] = total
    cnt_ref[...] = total


def _row_copy(src_hbm, src_row, dst, dst_row, sem):
    return pltpu.make_async_copy(src_hbm.at[pl.ds(src_row, 1)], dst.at[pl.ds(dst_row, 1)], sem)


def _for_each_row_copy(first, n_tokens, make_copy, act):
    def trip(jj, carry):
        j0 = first + pl.multiple_of(jj * ROWS_PER_TRIP, ROWS_PER_TRIP)
        for s in range(ROWS_PER_TRIP):
            for k in range(TOP_K):
                act(make_copy(j0 + s, k), k)
        return carry

    lax.fori_loop(0, n_tokens // ROWS_PER_TRIP, trip, 0)


def _run_row_copies(n_tokens, make_copy):
    def start(c):
        _for_each_row_copy(c * DMA_CHUNK, DMA_CHUNK, make_copy, lambda cp_, k: cp_.start(priority=k % 2))

    def wait(c):
        _for_each_row_copy(c * DMA_CHUNK, DMA_CHUNK, make_copy, lambda cp_, k: cp_.wait())

    n_chunks = n_tokens // DMA_CHUNK
    start(0)
    for c in range(n_chunks):
        if c + 1 < n_chunks:
            start(c + 1)
        wait(c)


def _dispatch_kernel(zblock_ref, slot_hbm, h_ref, xs_hbm, idx_ref, zero_ref, sem_idx, sem, sem_zero):
    td = TD_DISPATCH
    base = pl.program_id(0) * td

    @pl.when(pl.program_id(0) == 0)
    def _():
        zero_ref[...] = jnp.zeros_like(zero_ref)

        def zero_copy(e):
            z0 = pl.multiple_of(zblock_ref[e] * MOE_BLOCK, MOE_BLOCK)
            return pltpu.make_async_copy(zero_ref, xs_hbm.at[pl.ds(z0, MOE_BLOCK)], sem_zero)

        for act in ("start", "wait"):
            for e in range(zblock_ref.shape[0]):
                @pl.when(zblock_ref[e] >= 0)
                def _():
                    getattr(zero_copy(e), act)()

    cp = pltpu.make_async_copy(slot_hbm.at[pl.ds(base * TOP_K, td * TOP_K)], idx_ref, sem_idx)
    cp.start()
    cp.wait()

    def make_copy(j, k):
        return _row_copy(h_ref, j, xs_hbm, idx_ref[j * TOP_K + k], sem)

    _run_row_copies(td, make_copy)


def _dispatch_call(zblock, slot_flat, h2d, n_slots):
    n, d = h2d.shape
    td = TD_DISPATCH
    anyspec = pl.BlockSpec(memory_space=pl.ANY)
    grid_spec = pltpu.PrefetchScalarGridSpec(
        num_scalar_prefetch=1,
        grid=(n // td,),
        in_specs=[anyspec, pl.BlockSpec((td, d), lambda i, z: (i, 0))],
        out_specs=anyspec,
        scratch_shapes=[pltpu.SMEM((td * TOP_K,), jnp.int32),
                        pltpu.VMEM((MOE_BLOCK, d), F32),
                        pltpu.SemaphoreType.DMA(()), pltpu.SemaphoreType.DMA(()), pltpu.SemaphoreType.DMA(())],
    )
    return pl.pallas_call(
        _dispatch_kernel,
        grid_spec=grid_spec,
        out_shape=jax.ShapeDtypeStruct((n_slots, d), F32),
        compiler_params=_cparams(("arbitrary",)),
        name="moe_dispatch",
    )(zblock, slot_flat, h2d)


def _expert_kernel(layer, bexp_ref, nexp_ref, nused_ref, x_ref, wgu_hbm, bgu_ref, wd_hbm, bd_ref, y_ref,
                   wgu_f, wd_f, wgu_s, wd_s, sem_gu, sem_d):
    i = pl.program_id(0)
    dff = wd_s.shape[0]

    def weight_copies(e):
        return (pltpu.make_async_copy(wgu_hbm.at[layer, e], wgu_f, sem_gu),
                pltpu.make_async_copy(wd_hbm.at[layer, e], wd_f, sem_d))

    @pl.when(i < nused_ref[0])
    def _():
        e = bexp_ref[i]
        prev = bexp_ref[jnp.maximum(i - 1, 0)]

        @pl.when(i == 0)
        def _():
            for cp in weight_copies(e):
                cp.start()

        @pl.when((i == 0) | (e != prev))
        def _():
            for cp in weight_copies(e):
                cp.wait()
            wgu_s[...] = wgu_f[...].astype(BF16)
            wd_s[...] = wd_f[...].astype(BF16)
            nxt = nexp_ref[i]

            @pl.when(nxt >= 0)
            def _():
                for cp in weight_copies(nxt):
                    cp.start()

        xb = x_ref[...].astype(BF16)
        gu = _dot(xb, wgu_s[...]) + bgu_ref[0]
        glu = jnp.minimum(gu[:, :dff], SWIGLU_LIMIT)
        lin = jnp.clip(gu[:, dff:], -SWIGLU_LIMIT, SWIGLU_LIMIT)
        act = glu * (1.0 / (1.0 + jnp.exp(-SWIGLU_ALPHA * glu))) * (lin + 1.0)
        y_ref[...] = _dot(act.astype(BF16), wd_s[...]) + bd_ref[0]

    @pl.when(i >= nused_ref[0])
    def _():
        y_ref[...] = jnp.zeros_like(y_ref)


def _expert_call(layer, block_exp, next_exp, n_used, xs, w_gate_up, b_gate_up, w_down, b_down):
    n_slots, d = xs.shape
    _, ne, _, dff2 = w_gate_up.shape
    dff = dff2 // 2
    nb = n_slots // MOE_BLOCK

    def blk(i, bexp, nexp, nused):
        return jnp.minimum(i, nused[0] - 1)

    def bmap(i, bexp, nexp, nused):
        return (layer * ne + bexp[blk(i, bexp, nexp, nused)], 0, 0)

    anyspec = pl.BlockSpec(memory_space=pl.ANY)
    grid_spec = pltpu.PrefetchScalarGridSpec(
        num_scalar_prefetch=3,
        grid=(nb,),
        in_specs=[pl.BlockSpec((MOE_BLOCK, d), lambda i, bexp, nexp, nused: (blk(i, bexp, nexp, nused), 0)),
                  anyspec,
                  pl.BlockSpec((1, 1, dff2), bmap),
                  anyspec,
                  pl.BlockSpec((1, 1, d), bmap)],
        out_specs=pl.BlockSpec((MOE_BLOCK, d), lambda i, bexp, nexp, nused: (i, 0)),
        scratch_shapes=[pltpu.VMEM((d, dff2), F32), pltpu.VMEM((dff, d), F32),
                        pltpu.VMEM((d, dff2), BF16), pltpu.VMEM((dff, d), BF16),
                        pltpu.SemaphoreType.DMA(()), pltpu.SemaphoreType.DMA(())],
    )
    return pl.pallas_call(
        functools.partial(_expert_kernel, layer),
        grid_spec=grid_spec,
        out_shape=jax.ShapeDtypeStruct((n_slots, d), F32),
        compiler_params=_cparams(("arbitrary",)),
        name="moe_experts",
    )(block_exp, next_exp, n_used, xs, w_gate_up, b_gate_up.reshape(-1, 1, dff2), w_down,
      b_down.reshape(-1, 1, d))


def _combine_kernel(alpha, slot_hbm, ys_hbm, gate_ref, h_ref, g_ref, b_ref, o_ref, idx_ref, buf_ref, sem_idx, sem):
    tc = TC_COMBINE
    base = pl.program_id(0) * tc
    cp = pltpu.make_async_copy(slot_hbm.at[pl.ds(base * TOP_K, tc * TOP_K)], idx_ref, sem_idx)
    cp.start()
    cp.wait()

    def make_copy(j, k):
        return _row_copy(ys_hbm, idx_ref[j * TOP_K + k], buf_ref.at[k], j, sem)

    _run_row_copies(tc, make_copy)

    gate = gate_ref[...]
    y = alpha * h_ref[...]
    for k in range(TOP_K):
        y = y + buf_ref[k] * gate[:, k:k + 1]
    o_ref[...] = _layer_norm_rows(y, g_ref[...], b_ref[...])


def _combine_call(alpha, slot_flat, ys, gate_tok, h2d, g, b):
    n, d = h2d.shape
    tc = TC_COMBINE
    anyspec = pl.BlockSpec(memory_space=pl.ANY)
    return pl.pallas_call(
        functools.partial(_combine_kernel, alpha),
        grid=(n // tc,),
        in_specs=[anyspec, anyspec,
                  pl.BlockSpec((tc, TOP_K), lambda i: (i, 0)),
                  pl.BlockSpec((tc, d), lambda i: (i, 0)),
                  pl.BlockSpec((1, d), lambda i: (0, 0)),
                  pl.BlockSpec((1, d), lambda i: (0, 0))],
        out_specs=pl.BlockSpec((tc, d), lambda i: (i, 0)),
        out_shape=jax.ShapeDtypeStruct((n, d), F32),
        scratch_shapes=[pltpu.SMEM((tc * TOP_K,), jnp.int32),
                        pltpu.VMEM((TOP_K, tc, d), F32),
                        pltpu.SemaphoreType.DMA(()), pltpu.SemaphoreType.DMA(())],
        compiler_params=_cparams(("arbitrary",)),
        name="moe_combine_ln",
    )(slot_flat, ys, gate_tok, h2d, g.reshape(1, d), b.reshape(1, d))


def _moe_layer(layer, alpha, h2d, routing, w_gate_up, b_gate_up, w_down, b_down, g, b):
    n, d = h2d.shape
    exp_t, gate_t, rank_t, cnt = routing
    ne = cnt.shape[0]
    counts = cnt[:, 0].astype(jnp.int32)
    padded = (counts + MOE_BLOCK - 1) // MOE_BLOCK * MOE_BLOCK
    pend = jnp.cumsum(padded)
    pstart = pend - padded
    onehot = exp_t[:TOP_K, :, None] == jnp.arange(ne, dtype=jnp.int32)[None, None, :]
    slot_t = rank_t[:TOP_K] + jnp.sum(jnp.where(onehot, pstart[None, None, :], 0), axis=-1)
    slot_flat = slot_t.T.reshape(-1)
    nb = n * TOP_K // MOE_BLOCK + ne
    n_used = (pend[-1] // MOE_BLOCK).astype(jnp.int32).reshape(1)
    bstart = jnp.arange(nb, dtype=jnp.int32) * MOE_BLOCK
    block_exp = jnp.minimum(jnp.sum(pend[None, :] <= bstart[:, None], axis=1), ne - 1).astype(jnp.int32)

    n_slots = nb * MOE_BLOCK
    last_blk = jnp.where(padded > 0, pend // MOE_BLOCK - 1, -1)
    tail = n_used[0] + jnp.arange(nb - n * TOP_K // MOE_BLOCK, dtype=jnp.int32)
    zblock = jnp.concatenate([last_blk, jnp.where(tail < nb, tail, -1)]).astype(jnp.int32)
    xs = _dispatch_call(zblock, slot_flat, h2d, n_slots)
    eid = jnp.arange(ne, dtype=jnp.int32)
    later_used = (padded[None, :] > 0) & (eid[None, :] > eid[:, None])
    nxt = jnp.min(jnp.where(later_used, eid[None, :], ne), axis=1)
    nxt = jnp.where(nxt == ne, -1, nxt)
    next_exp = jnp.sum(jnp.where(block_exp[:, None] == eid[None, :], nxt[None, :], 0), axis=1).astype(jnp.int32)
    ys = _expert_call(layer, block_exp, next_exp, n_used, xs, w_gate_up, b_gate_up, w_down, b_down)
    return _combine_call(alpha, slot_flat, ys, gate_t[:TOP_K].T, h2d, g, b)


def kernel(x, positions, ln_in_g, ln_in_b, w_in, b_forget, diff_lambda, diff_subln_g, mla_q_norm_g, mla_w_uq, mla_kv_norm_g, mla_w_ukv, w_out, ln1_g, ln1_b, router_w, router_b, w_gate_up, b_gate_up, w_down, b_down, ln2_g, ln2_b):
    bsz, s, d = x.shape
    depth = w_in.shape[0]
    n = bsz * s
    alpha = (2 * depth) ** 0.25
    cos_t, sin_t = _rope_tables(positions)
    h = _layer_norm_call(x.reshape(n, d), ln_in_g, ln_in_b)
    g_sub = jnp.broadcast_to(diff_subln_g[:, :, None], (depth, DIFF_V_DIM, LANES))
    for l in range(depth):
        lambda_init = 0.8 - 0.6 * math.exp(-0.3 * l)
        pw = _prep_proj_weights(w_in[l], b_forget[l], mla_q_norm_g[l], mla_w_uq[l], mla_kv_norm_g[l], mla_w_ukv[l])
        (sbq, sbk, sbv, dfq, dfk, dfv, fxq, fxk, fxv, mq, mk, mv) = _proj_call(h.reshape(bsz, s, d), pw, cos_t, sin_t)
        mix_a = _attn_call("sb", "sb", lambda_init, sbq, sbk, sbv)
        mix_b = _attn_call("diff", "diff", lambda_init, dfq, dfk, dfv, (diff_lambda[l:l + 1], g_sub[l]))
        mix_c = _attn_call("fox", "wide", lambda_init, fxq, fxk, fxv)
        mix_d = _attn_call("mla", "wide", lambda_init, mq, mk, mv)
        mixes = [m.reshape(n, 256) for m in (mix_a, mix_b, mix_c, mix_d)]
        h, *routing = _out_call(alpha, mixes, w_out[l].astype(BF16), h, ln1_g[l], ln1_b[l], router_w[l], router_b[l])
        h = _moe_layer(l, alpha, h, routing, w_gate_up, b_gate_up, w_down, b_down, ln2_g[l], ln2_b[l])
    return h.reshape(bsz, s, d)
```
